```python
import math
import jax, jax.numpy as jnp
from jax import lax
import numpy as np

D_MODEL = 2048
BATCH = 4
SEQ = 2048
DEPTH = 2
DEC_BATCH = 128
DEC_SEQ = 4
PAST_LEN = 16384
PAGE_SIZE = 128

N_BRANCH = 3
D_BR = D_MODEL // 2
D_A = D_BR
H_A = 4
DH_A = D_A // H_A
CHUNK = 128
D_CONV = D_BR
CONV_K = 31
D_POOL = D_BR
POOL_WINDOWS = (2, 4, 8, 16)
N_POOL = len(POOL_WINDOWS)
G_POOL = D_POOL // N_POOL
POOL_PAST = max(POOL_WINDOWS) - 1
MEM_LEN = 256
X_HEADS = 4
X_HEAD_DIM = D_MODEL // X_HEADS
D_FF = 4 * D_MODEL
D_IN = 2 * D_A + 2 * D_CONV + D_POOL + N_BRANCH * D_MODEL
SPLITS = (D_A, 2 * D_A, 2 * D_A + 2 * D_CONV, 2 * D_A + 2 * D_CONV + D_POOL)
RMS_EPS = 1e-6
LN_EPS = 1e-5

kernel_name = "gated_hybrid_chunkmlp_conv_pool_decoder_step"


def rms_norm(x, g):
    xf = x.astype(jnp.float32)
    y = xf * lax.rsqrt(jnp.mean(xf * xf, axis=-1, keepdims=True) + RMS_EPS)
    return (y * g.astype(jnp.float32)).astype(x.dtype)


def layer_norm(x, g, b):
    xf = x.astype(jnp.float32)
    mu = jnp.mean(xf, axis=-1, keepdims=True)
    xc = xf - mu
    var = jnp.mean(xc * xc, axis=-1, keepdims=True)
    y = xc * lax.rsqrt(var + LN_EPS) * g.astype(jnp.float32) + b.astype(jnp.float32)
    return y.astype(x.dtype)


def chunk_mixer(u, v, ln_g, ln_b, w_s, b_s):
    u = jax.nn.gelu(u)
    v = layer_norm(jax.nn.gelu(v), ln_g, ln_b)
    B, S, _ = v.shape
    L = min(S, CHUNK)
    vh = v.reshape(B, S // L, L, H_A, DH_A)
    w = jnp.tril(w_s[:, :L, :L]).astype(v.dtype)
    mixed = jnp.einsum('hts,bcshd->bcthd', w, vh)
    mixed = mixed + jnp.transpose(b_s[:, :L]).astype(v.dtype)[None, None, :, :, None]
    return u * mixed.reshape(B, S, D_A), v


def conv_module(c, past, conv_w, conv_b, ln_g, ln_b):
    a, gt = jnp.split(c, 2, axis=-1)
    glu = a * jax.nn.sigmoid(gt)
    full = jnp.concatenate([past.astype(glu.dtype), glu], axis=1)
    out = lax.conv_general_dilated(
        full, conv_w.astype(full.dtype)[:, None, :], window_strides=(1,), padding='VALID',
        dimension_numbers=('NWC', 'WIO', 'NWC'), feature_group_count=D_CONV)
    out = out + conv_b.astype(out.dtype)
    y = jax.nn.silu(layer_norm(out, ln_g, ln_b))
    return y, full[:, -(CONV_K - 1):]


def pool_mixer(p, past, pos0, pool_w, pool_scale):
    B, S, _ = p.shape
    P = POOL_PAST
    full = jnp.concatenate([past.astype(p.dtype), p], axis=1)
    cs = jnp.cumsum(full.astype(jnp.float32), axis=1)
    cs = jnp.pad(cs, ((0, 0), (1, 0), (0, 0)))
    end = cs[:, P + 1:P + 1 + S]
    pos = pos0 + jnp.arange(S)
    outs = []
    for gi, w in enumerate(POOL_WINDOWS):
        sl = slice(gi * G_POOL, (gi + 1) * G_POOL)
        start = cs[:, P + 1 - w:P + 1 - w + S, sl]
        cnt = jnp.minimum(w, pos + 1).astype(jnp.float32)[None, :, None]
        outs.append((end[..., sl] - start) / cnt)
    pooled = jnp.stack(outs, axis=2)
    mixed = (pooled - p.reshape(B, S, N_POOL, G_POOL).astype(jnp.float32)).astype(p.dtype)
    y = jnp.einsum('bsgc,gcd->bsgd', mixed, pool_w.astype(p.dtype)).reshape(B, S, D_POOL)
    return y * pool_scale.astype(y.dtype), full[:, -P:]


def mem_kv(mem, g_mem, w_k, w_v):
    B, M, _ = mem.shape
    m = rms_norm(mem, g_mem)
    k = (m @ w_k).reshape(B, M, X_HEADS, X_HEAD_DIM)
    v = (m @ w_v).reshape(B, M, X_HEADS, X_HEAD_DIM)
    return k, v


def cross_attn(h, k, v, w_q, w_o):
    B, S, _ = h.shape
    q = (h @ w_q).reshape(B, S, X_HEADS, X_HEAD_DIM)
    s = jnp.einsum('bqhd,bkhd->bhqk', q, k.astype(q.dtype)).astype(jnp.float32) * (X_HEAD_DIM ** -0.5)
    pr = jax.nn.softmax(s, axis=-1).astype(q.dtype)
    o = jnp.einsum('bhqk,bkhd->bqhd', pr, v.astype(q.dtype)).reshape(B, S, D_MODEL)
    return o @ w_o


def decoder_layer(x, mk, mv, conv_past, pool_past, pos0,
                  g_mix, w_in, ln_v_g, ln_v_b, w_s, b_s, conv_w, conv_b, ln_c_g, ln_c_b,
                  pool_w, pool_scale, w_branch, w_out, g_xattn, w_xq, w_xo, g_mlp, w_up, w_down):
    B, S, _ = x.shape
    h = rms_norm(x, g_mix)
    z = h @ w_in
    u, v, c, p, gate = jnp.split(z, SPLITS, axis=-1)
    y_a, v_state = chunk_mixer(u, v, ln_v_g, ln_v_b, w_s, b_s)
    y_b, conv_state = conv_module(c, conv_past, conv_w, conv_b, ln_c_g, ln_c_b)
    y_c, pool_state = pool_mixer(p, pool_past, pos0, pool_w, pool_scale)
    ys = jnp.stack([y_a, y_b, y_c], axis=2)
    br = jnp.einsum('bsnc,ncd->bsnd', ys, w_branch)
    gates = jax.nn.sigmoid(gate.reshape(B, S, N_BRANCH, D_MODEL))
    merged = jnp.sum(gates * br, axis=2)
    x = x + merged @ w_out
    x = x + cross_attn(rms_norm(x, g_xattn), mk, mv, w_xq, w_xo)
    hm = rms_norm(x, g_mlp)
    x = x + jnp.square(jax.nn.relu(hm @ w_up)) @ w_down
    return x, v_state, conv_state, pool_state


def setup_inputs(seed: int = 0) -> dict:
    key = jax.random.key(seed)
    ks = iter(jax.random.split(key, 40))

    def nrm(shape, scale):
        return jax.random.normal(next(ks), shape, jnp.float32) * scale

    def gain(shape):
        return 1.0 + nrm(shape, 0.05)

    return {
        "x_prompt": nrm((BATCH, SEQ, D_MODEL), 1.0),
        "x_sample": nrm((DEC_BATCH, DEC_SEQ, D_MODEL), 1.0),
        "mem_prompt": nrm((BATCH, MEM_LEN, D_MODEL), 1.0),
        "cache_mem_k": nrm((DEPTH, DEC_BATCH, MEM_LEN, X_HEADS, X_HEAD_DIM), 1.0),
        "cache_mem_v": nrm((DEPTH, DEC_BATCH, MEM_LEN, X_HEADS, X_HEAD_DIM), 1.0),
        "state_conv": nrm((DEPTH, DEC_BATCH, CONV_K - 1, D_CONV), 0.5),
        "state_pool": nrm((DEPTH, DEC_BATCH, POOL_PAST, D_POOL), 1.0),
        "g_mix": gain((DEPTH, D_MODEL)),
        "w_in": nrm((DEPTH, D_MODEL, D_IN), D_MODEL ** -0.5),
        "ln_v_g": gain((DEPTH, D_A)),
        "ln_v_b": nrm((DEPTH, D_A), 0.02),
        "w_s": nrm((DEPTH, H_A, CHUNK, CHUNK), CHUNK ** -0.5),
        "b_s": 1.0 + nrm((DEPTH, H_A, CHUNK), 0.1),
        "conv_w": nrm((DEPTH, CONV_K, D_CONV), CONV_K ** -0.5),
        "conv_b": nrm((DEPTH, D_CONV), 0.02),
        "ln_c_g": gain((DEPTH, D_CONV)),
        "ln_c_b": nrm((DEPTH, D_CONV), 0.02),
        "pool_w": nrm((DEPTH, N_POOL, G_POOL, G_POOL), G_POOL ** -0.5),
        "pool_scale": 1.0 + nrm((DEPTH, D_POOL), 0.1),
        "w_branch": nrm((DEPTH, N_BRANCH, D_BR, D_MODEL), D_BR ** -0.5),
        "w_out": nrm((DEPTH, D_MODEL, D_MODEL), D_MODEL ** -0.5),
        "g_xattn": gain((DEPTH, D_MODEL)),
        "g_mem": gain((DEPTH, D_MODEL)),
        "w_xq": nrm((DEPTH, D_MODEL, D_MODEL), D_MODEL ** -0.5),
        "w_xk": nrm((DEPTH, D_MODEL, D_MODEL), D_MODEL ** -0.5),
        "w_xv": nrm((DEPTH, D_MODEL, D_MODEL), D_MODEL ** -0.5),
        "w_xo": nrm((DEPTH, D_MODEL, D_MODEL), D_MODEL ** -0.5),
        "g_mlp": gain((DEPTH, D_MODEL)),
        "w_up": nrm((DEPTH, D_MODEL, D_FF), D_MODEL ** -0.5),
        "w_down": nrm((DEPTH, D_FF, D_MODEL), D_FF ** -0.5),
        "g_final": gain((D_MODEL,)),
    }


def reference(x_prompt, x_sample, mem_prompt, cache_mem_k, cache_mem_v, state_conv, state_pool,
              g_mix, w_in, ln_v_g, ln_v_b, w_s, b_s, conv_w, conv_b, ln_c_g, ln_c_b,
              pool_w, pool_scale, w_branch, w_out, g_xattn, g_mem, w_xq, w_xk, w_xv, w_xo,
              g_mlp, w_up, w_down, g_final):
    xp, xs = x_prompt, x_sample
    Bp = xp.shape[0]
    mk_p, mv_p, conv_p, pool_p = [], [], [], []
    conv_s, pool_s, chunk_s = [], [], []
    for l in range(DEPTH):
        lw = (g_mix[l], w_in[l], ln_v_g[l], ln_v_b[l], w_s[l], b_s[l], conv_w[l], conv_b[l],
              ln_c_g[l], ln_c_b[l], pool_w[l], pool_scale[l], w_branch[l], w_out[l],
              g_xattn[l], w_xq[l], w_xo[l], g_mlp[l], w_up[l], w_down[l])
        mk, mv = mem_kv(mem_prompt, g_mem[l], w_xk[l], w_xv[l])
        zc = jnp.zeros((Bp, CONV_K - 1, D_CONV), xp.dtype)
        zp = jnp.zeros((Bp, POOL_PAST, D_POOL), xp.dtype)
        xp, _, cp, pp = decoder_layer(xp, mk, mv, zc, zp, 0, *lw)
        mk_p.append(mk); mv_p.append(mv); conv_p.append(cp); pool_p.append(pp)
        xs, vs, cs, ps = decoder_layer(xs, cache_mem_k[l], cache_mem_v[l], state_conv[l],
                                       state_pool[l], PAST_LEN, *lw)
        conv_s.append(cs); pool_s.append(ps); chunk_s.append(vs)
    y_prompt = rms_norm(xp, g_final)
    y_sample = rms_norm(xs, g_final)
    new_mem_k_prompt = jnp.stack(mk_p)
    new_mem_v_prompt = jnp.stack(mv_p)
    new_conv_prompt = jnp.stack(conv_p)
    new_pool_prompt = jnp.stack(pool_p)
    new_conv_sample = jnp.stack(conv_s)
    new_pool_sample = jnp.stack(pool_s)
    new_chunk_v_sample = jnp.stack(chunk_s)
    return (y_prompt, y_sample, new_mem_k_prompt, new_mem_v_prompt, new_conv_prompt,
            new_pool_prompt, new_conv_sample, new_pool_sample, new_chunk_v_sample)
```

```python
import functools

import jax
import jax.numpy as jnp
from jax import lax
from jax.experimental import pallas as pl
from jax.experimental.pallas import tpu as pltpu

D_MODEL = 2048
BATCH = 4
SEQ = 2048
DEPTH = 2
DEC_BATCH = 128
DEC_SEQ = 4
PAST_LEN = 16384
D_BR = D_MODEL // 2
D_A = D_BR
H_A = 4
DH_A = D_A // H_A
CHUNK = 128
D_CONV = D_BR
CONV_K = 31
D_POOL = D_BR
POOL_WINDOWS = (2, 4, 8, 16)
N_POOL = len(POOL_WINDOWS)
G_POOL = D_POOL // N_POOL
POOL_PAST = max(POOL_WINDOWS) - 1
MEM_LEN = 256
X_HEADS = 4
X_HEAD_DIM = D_MODEL // X_HEADS
D_FF = 4 * D_MODEL
N_BRANCH = 3
D_IN = 2 * D_A + 2 * D_CONV + D_POOL + N_BRANCH * D_MODEL
RMS_EPS = 1e-6
LN_EPS = 1e-5

T_P = BATCH * SEQ
T_S = DEC_BATCH * DEC_SEQ
T = T_P + T_S
SB = 16
SBLK = DEC_SEQ * SB
N_SBLK = DEC_BATCH // SB

VMEM_PHYSICAL_BYTES = 64 * 1024 * 1024
VMEM_CEILING_BYTES = VMEM_PHYSICAL_BYTES - 6 * 1024 * 1024

F32 = jnp.float32
BF16 = jnp.bfloat16


def _nbytes(shape, dtype):
    n = 1
    for s in shape:
        n *= s
    return n * jnp.dtype(dtype).itemsize


def _params(semantics, pipelined, resident=0, temps=0):
    need = 2 * sum(_nbytes(s, d) for s, d in pipelined) + resident + temps
    need = need + need // 8 + (2 << 20)
    return pltpu.CompilerParams(
        dimension_semantics=semantics,
        vmem_limit_bytes=int(min(max(need, 16 << 20), VMEM_CEILING_BYTES)))


def _resident(shape, index_map):
    return pl.BlockSpec(shape, index_map, pipeline_mode=pl.Buffered(1))


def _sigmoid(x):
    return 1.0 / (1.0 + jnp.exp(-x))


def _rms(x, g):
    return x * lax.rsqrt(jnp.mean(x * x, axis=-1, keepdims=True) + RMS_EPS) * g


def _ln(x, g, b):
    mu = jnp.mean(x, axis=-1, keepdims=True)
    xc = x - mu
    var = jnp.mean(xc * xc, axis=-1, keepdims=True)
    return xc * lax.rsqrt(var + LN_EPS) * g + b


def _dot(a, b):
    return jnp.dot(a, b, preferred_element_type=F32)


def _norm_kernel(x_ref, g_ref, o_ref):
    o_ref[...] = _rms(x_ref[...], g_ref[...]).astype(o_ref.dtype)


def _norm_call(x, g, tm=512):
    n = x.shape[0]
    blk = (tm, D_MODEL)
    return pl.pallas_call(
        _norm_kernel,
        out_shape=jax.ShapeDtypeStruct((n, D_MODEL), BF16),
        grid=(n // tm,),
        in_specs=[pl.BlockSpec(blk, lambda i: (i, 0)),
                  pl.BlockSpec((1, D_MODEL), lambda i: (0, 0))],
        out_specs=pl.BlockSpec(blk, lambda i: (i, 0)),
        compiler_params=_params(("parallel",), [(blk, F32), (blk, BF16)],
                                temps=2 * _nbytes(blk, F32)),
        name="rmsnorm_in",
    )(x, g)


def _proj_kernel(mode, *refs):
    if mode == "glu":
        h_ref, wa_ref, wg_ref, o_ref = refs
        h = h_ref[...]
        a = _dot(h, wa_ref[...])
        gt = _dot(h, wg_ref[...])
        o_ref[...] = (a * _sigmoid(gt)).astype(o_ref.dtype)
        return
    if mode == "gelu_ln":
        h_ref, w_ref, g_ref, b_ref, o_ref = refs
    else:
        h_ref, w_ref, o_ref = refs
    z = _dot(h_ref[...], w_ref[...])
    if mode == "gelu":
        z = jax.nn.gelu(z)
    elif mode == "gelu_ln":
        z = _ln(jax.nn.gelu(z), g_ref[...], b_ref[...])
    elif mode == "sigmoid":
        z = _sigmoid(z)
    o_ref[...] = z.astype(o_ref.dtype)


def _proj_call(h, w, *, col0, ncols, mode, out_dtype, name, extra=(), tm=1088, tn=1024):
    n = h.shape[0]
    k = h.shape[1]
    hblk, wblk, oblk = (tm, k), (k, tn), (tm, tn)
    in_specs = [pl.BlockSpec(hblk, lambda j, i: (i, 0)),
                pl.BlockSpec(wblk, lambda j, i: (0, col0 + j))]
    args = [h, w]
    pipelined = [(hblk, h.dtype), (wblk, w.dtype), (oblk, out_dtype)]
    if mode == "glu":
        in_specs.append(pl.BlockSpec(wblk, lambda j, i: (0, col0 + ncols + j)))
        args.append(w)
        pipelined.append((wblk, w.dtype))
    for e in extra:
        in_specs.append(pl.BlockSpec((1, tn), lambda j, i: (0, 0)))
        args.append(e)
    return pl.pallas_call(
        functools.partial(_proj_kernel, mode),
        out_shape=jax.ShapeDtypeStruct((n, ncols * tn), out_dtype),
        grid=(ncols, n // tm),
        in_specs=in_specs,
        out_specs=pl.BlockSpec(oblk, lambda j, i: (i, j)),
        compiler_params=_params(("parallel", "parallel"), pipelined,
                                temps=3 * _nbytes(oblk, F32)),
        name=name,
    )(*args)


MIX_TM = 256
MIX_RB = 32
CONV_HALO = 32
POOL_HALO = 16


def _conv_ln_silu(window, cw_ref, cb, lg, lb):
    acc = cb + cw_ref[0:1, :] * window(0)
    for k in range(1, CONV_K):
        acc = acc + cw_ref[k:k + 1, :] * window(k)
    y = _ln(acc, lg, lb)
    return y * _sigmoid(y)


def _mix_prompt_kernel(ug_ref, vn_ref, glu_ref, gh_ref, p_ref, ph_ref,
                       ws_ref, bs_ref, cw_ref, cb_ref, lg_ref, lb_ref, pw_ref, psc_ref,
                       ya_ref, yb_ref, yc_ref, gwin, pwin):
    tm = MIX_TM
    seq_tile = pl.program_id(0) % (SEQ // tm)
    first = seq_tile == 0

    row = lax.broadcasted_iota(jnp.int32, (CHUNK, CHUNK), 0)
    col = lax.broadcasted_iota(jnp.int32, (CHUNK, CHUNK), 1)
    for h in range(H_A):
        w = jnp.where(row >= col, ws_ref[h], 0.0).astype(BF16)
        hs = slice(h * DH_A, (h + 1) * DH_A)
        for c in range(tm // CHUNK):
            rs = slice(c * CHUNK, (c + 1) * CHUNK)
            mixed = _dot(w, vn_ref[rs, hs].astype(BF16)) + bs_ref[:, hs]
            ya_ref[rs, hs] = (ug_ref[rs, hs] * mixed).astype(ya_ref.dtype)

    gwin[0:CONV_HALO, :] = jnp.where(first, 0.0, gh_ref[...])
    gwin[CONV_HALO:, :] = glu_ref[...]
    cb, lg, lb = cb_ref[...], lg_ref[...], lb_ref[...]
    off = CONV_HALO - (CONV_K - 1)
    for rb in range(tm // MIX_RB):
        r0 = rb * MIX_RB
        y = _conv_ln_silu(lambda k: gwin[r0 + off + k:r0 + off + k + MIX_RB, :],
                          cw_ref, cb, lg, lb)
        yb_ref[r0:r0 + MIX_RB, :] = y.astype(yb_ref.dtype)

    pwin[0:POOL_HALO, :] = jnp.where(first, 0.0, ph_ref[...])
    pwin[POOL_HALO:, :] = p_ref[...]
    pos = seq_tile * tm + lax.broadcasted_iota(jnp.int32, (tm, 1), 0)
    for g, wlen in enumerate(POOL_WINDOWS):
        gs = slice(g * G_POOL, (g + 1) * G_POOL)
        s = pwin[POOL_HALO:POOL_HALO + tm, gs]
        for d in range(1, wlen):
            s = s + pwin[POOL_HALO - d:POOL_HALO - d + tm, gs]
        inv = 1.0 / jnp.minimum(wlen, pos + 1).astype(F32)
        mixed = s * inv - p_ref[:, gs]
        y = _dot(mixed.astype(BF16), pw_ref[g]) * psc_ref[:, gs]
        yc_ref[:, gs] = y.astype(yc_ref.dtype)


def _mix_prompt_call(ug, vn, glu, p, ws, bs_full, cw, cb, lg, lb, pw, psc):
    tm = MIX_TM
    blk = (tm, D_BR)
    row_map = lambda i: (i, 0)
    const2 = lambda i: (0, 0)
    const3 = lambda i: (0, 0, 0)
    in_specs = [
        pl.BlockSpec(blk, row_map),
        pl.BlockSpec(blk, row_map),
        pl.BlockSpec(blk, row_map),
        pl.BlockSpec((CONV_HALO, D_BR),
                     lambda i: (jnp.maximum(i * (tm // CONV_HALO) - 1, 0), 0)),
        pl.BlockSpec(blk, row_map),
        pl.BlockSpec((POOL_HALO, D_BR),
                     lambda i: (jnp.maximum(i * (tm // POOL_HALO) - 1, 0), 0)),
        pl.BlockSpec((H_A, CHUNK, CHUNK), const3),
        pl.BlockSpec((CHUNK, D_A), const2),
        pl.BlockSpec((CONV_K, D_CONV), const2),
        pl.BlockSpec((1, D_CONV), const2),
        pl.BlockSpec((1, D_CONV), const2),
        pl.BlockSpec((1, D_CONV), const2),
        pl.BlockSpec((N_POOL, G_POOL, G_POOL), const3),
        pl.BlockSpec((1, D_POOL), const2),
    ]
    out = jax.ShapeDtypeStruct((T, D_BR), BF16)
    return pl.pallas_call(
        _mix_prompt_kernel,
        out_shape=(out, out, out),
        grid=(T_P // tm,),
        in_specs=in_specs,
        out_specs=(pl.BlockSpec(blk, row_map),) * 3,
        scratch_shapes=[pltpu.VMEM((CONV_HALO + tm, D_CONV), F32),
                        pltpu.VMEM((POOL_HALO + tm, D_POOL), F32)],
        compiler_params=_params(
            ("parallel",),
            [(blk, F32)] * 4 + [(blk, BF16)] * 3 + [((CONV_HALO + POOL_HALO, D_BR), F32),
                                                    ((CHUNK + CONV_K + 8, D_BR), F32)],
            resident=_nbytes((CONV_HALO + POOL_HALO + 2 * tm, D_BR), F32),
            temps=20 * _nbytes(blk, F32)),
        name="mix_prompt",
    )(ug, vn, glu, glu, p, p, ws, bs_full, cw, cb, lg, lb, pw, psc)


def _mix_sample_kernel(ug_ref, vn_ref, glu_ref, p_ref, cpast_ref, ppast_ref,
                       wsm_ref, bsm_ref, cw_ref, cb_ref, lg_ref, lb_ref, pw_ref, psc_ref,
                       ya_in, yb_in, yc_in, ya_ref, yb_ref, yc_ref, mix_scr):
    del ya_in, yb_in, yc_in
    cb, lg, lb = cb_ref[...], lg_ref[...], lb_ref[...]

    def rows(t):
        return slice(t * SB, (t + 1) * SB)

    def conv_src(j):
        return cpast_ref[j] if j < CONV_K - 1 else glu_ref[rows(j - (CONV_K - 1)), :]

    def pool_src(j, gs):
        return ppast_ref[j, :, gs] if j < POOL_PAST else p_ref[rows(j - POOL_PAST), gs]

    for t in range(DEC_SEQ):
        m = bsm_ref[t:t + 1, :] + wsm_ref[t, 0:1, :] * vn_ref[rows(0), :]
        for s in range(1, t + 1):
            m = m + wsm_ref[t, s:s + 1, :] * vn_ref[rows(s), :]
        ya_ref[rows(t), :] = (ug_ref[rows(t), :] * m).astype(ya_ref.dtype)
        y = _conv_ln_silu(lambda k: conv_src(t + k), cw_ref, cb, lg, lb)
        yb_ref[rows(t), :] = y.astype(yb_ref.dtype)
        for g, wlen in enumerate(POOL_WINDOWS):
            gs = slice(g * G_POOL, (g + 1) * G_POOL)
            s = pool_src(POOL_PAST + t, gs)
            for d in range(1, wlen):
                s = s + pool_src(POOL_PAST + t - d, gs)
            cnt = float(min(wlen, PAST_LEN + t + 1))
            mix_scr[rows(t), gs] = s / cnt - p_ref[rows(t), gs]
    for g in range(N_POOL):
        gs = slice(g * G_POOL, (g + 1) * G_POOL)
        y = _dot(mix_scr[:, gs].astype(BF16), pw_ref[g]) * psc_ref[:, gs]
        yc_ref[:, gs] = y.astype(yc_ref.dtype)


def _mix_sample_call(ug, vn, glu, p, cpast, ppast, wsm, bsm, cw, cb, lg, lb, pw, psc, ya, yb, yc):
    blk = (SBLK, D_BR)
    blk0 = T_P // SBLK
    row_map = lambda j: (blk0 + j, 0)
    const2 = lambda j: (0, 0)
    const3 = lambda j: (0, 0, 0)
    any_spec = pl.BlockSpec(memory_space=pl.ANY)
    in_specs = [
        pl.BlockSpec(blk, row_map), pl.BlockSpec(blk, row_map),
        pl.BlockSpec(blk, row_map), pl.BlockSpec(blk, row_map),
        pl.BlockSpec((CONV_K - 1, SB, D_CONV), lambda j: (0, j, 0)),
        pl.BlockSpec((POOL_PAST, SB, D_POOL), lambda j: (0, j, 0)),
        pl.BlockSpec((DEC_SEQ, DEC_SEQ, D_A), const3),
        pl.BlockSpec((DEC_SEQ, D_A), const2),
        pl.BlockSpec((CONV_K, D_CONV), const2),
        pl.BlockSpec((1, D_CONV), const2),
        pl.BlockSpec((1, D_CONV), const2),
        pl.BlockSpec((1, D_CONV), const2),
        pl.BlockSpec((N_POOL, G_POOL, G_POOL), const3),
        pl.BlockSpec((1, D_POOL), const2),
        any_spec, any_spec, any_spec,
    ]
    out = jax.ShapeDtypeStruct((T, D_BR), BF16)
    return pl.pallas_call(
        _mix_sample_kernel,
        out_shape=(out, out, out),
        grid=(N_SBLK,),
        in_specs=in_specs,
        out_specs=(pl.BlockSpec(blk, row_map),) * 3,
        scratch_shapes=[pltpu.VMEM(blk, F32)],
        input_output_aliases={14: 0, 15: 1, 16: 2},
        compiler_params=_params(
            ("parallel",),
            [(blk, F32)] * 4 + [(blk, BF16)] * 3
            + [((CONV_K - 1 + POOL_PAST, SB, D_BR), F32), ((CONV_K + 16, D_BR), F32)],
            resident=_nbytes(blk, F32), temps=32 * _nbytes(blk, F32)),
        name="mix_sample",
    )(ug, vn, glu, p, cpast, ppast, wsm, bsm, cw, cb, lg, lb, pw, psc, ya, yb, yc)


def _merge_kernel(ya_ref, yb_ref, yc_ref, g0_ref, g1_ref, g2_ref, x_ref,
                  wb_ref, wo_ref, gn_ref, xo_ref, ho_ref):
    merged = g0_ref[...] * _dot(ya_ref[...], wb_ref[0])
    merged = merged + g1_ref[...] * _dot(yb_ref[...], wb_ref[1])
    merged = merged + g2_ref[...] * _dot(yc_ref[...], wb_ref[2])
    xn = x_ref[...] + _dot(merged.astype(BF16), wo_ref[...])
    xo_ref[...] = xn
    ho_ref[...] = _rms(xn, gn_ref[...]).astype(ho_ref.dtype)


def _merge_call(ya, yb, yc, gates, x, wb, wo, gn, tm=256):
    yblk, xblk = (tm, D_BR), (tm, D_MODEL)
    row_map = lambda i: (i, 0)
    in_specs = [
        pl.BlockSpec(yblk, row_map), pl.BlockSpec(yblk, row_map), pl.BlockSpec(yblk, row_map),
        pl.BlockSpec(xblk, lambda i: (i, 0)),
        pl.BlockSpec(xblk, lambda i: (i, 1)),
        pl.BlockSpec(xblk, lambda i: (i, 2)),
        pl.BlockSpec(xblk, row_map),
        _resident((N_BRANCH, D_BR, D_MODEL), lambda i: (0, 0, 0)),
        _resident((D_MODEL, D_MODEL), lambda i: (0, 0)),
        pl.BlockSpec((1, D_MODEL), lambda i: (0, 0)),
    ]
    return pl.pallas_call(
        _merge_kernel,
        out_shape=(jax.ShapeDtypeStruct((T, D_MODEL), F32),
                   jax.ShapeDtypeStruct((T, D_MODEL), BF16)),
        grid=(T // tm,),
        in_specs=in_specs,
        out_specs=(pl.BlockSpec(xblk, row_map), pl.BlockSpec(xblk, row_map)),
        compiler_params=_params(
            ("parallel",),
            [(yblk, BF16)] * 3 + [(xblk, gates.dtype)] * 3 + [(xblk, F32)] * 2 + [(xblk, BF16)],
            resident=_nbytes((N_BRANCH, D_BR, D_MODEL), BF16) + _nbytes((D_MODEL, D_MODEL), BF16),
            temps=3 * _nbytes(xblk, F32)),
        name="merge_out",
    )(ya, yb, yc, gates, gates, gates, x, wb, wo, gn)


def _memkv_kernel(m_ref, g_ref, wk_ref, wv_ref, k32_ref, v32_ref, k16_ref, v16_ref):
    m = _rms(m_ref[...], g_ref[...]).astype(BF16)
    k = _dot(m, wk_ref[...])
    v = _dot(m, wv_ref[...])
    k32_ref[...] = k
    v32_ref[...] = v
    k16_ref[...] = k.astype(BF16)
    v16_ref[...] = v.astype(BF16)


def _memkv_call(mem, g, wk, wv, tn=1024):
    n = mem.shape[0]
    mblk, wblk, oblk = (n, D_MODEL), (D_MODEL, tn), (n, tn)
    o32 = jax.ShapeDtypeStruct((n, D_MODEL), F32)
    o16 = jax.ShapeDtypeStruct((n, D_MODEL), BF16)
    col = lambda j: (0, j)
    return pl.pallas_call(
        _memkv_kernel,
        out_shape=(o32, o32, o16, o16),
        grid=(D_MODEL // tn,),
        in_specs=[_resident(mblk, lambda j: (0, 0)),
                  pl.BlockSpec((1, D_MODEL), lambda j: (0, 0)),
                  pl.BlockSpec(wblk, col), pl.BlockSpec(wblk, col)],
        out_specs=(pl.BlockSpec(oblk, col),) * 4,
        compiler_params=_params(
            ("parallel",),
            [(wblk, BF16)] * 2 + [(oblk, F32)] * 2 + [(oblk, BF16)] * 2,
            resident=_nbytes(mblk, F32), temps=2 * _nbytes(mblk, F32)),
        name="mem_kv",
    )(mem, g, wk, wv)


_NT = (((1,), (1,)), ((), ()))


def _softmax_rows(s):
    e = jnp.exp(s - jnp.max(s, axis=-1, keepdims=True))
    return e * (1.0 / jnp.sum(e, axis=-1, keepdims=True))


def _xattn_prompt_kernel(q_ref, k_ref, v_ref, o_ref):
    scale = X_HEAD_DIM ** -0.5
    for h in range(X_HEADS):
        hs = slice(h * X_HEAD_DIM, (h + 1) * X_HEAD_DIM)
        s = lax.dot_general(q_ref[:, hs], k_ref[:, hs], _NT, preferred_element_type=F32) * scale
        pr = _softmax_rows(s).astype(BF16)
        o_ref[:, hs] = _dot(pr, v_ref[:, hs]).astype(o_ref.dtype)


def _xattn_prompt_call(q, k16, v16, tq=512):
    qblk, kblk = (tq, D_MODEL), (MEM_LEN, D_MODEL)
    per_seq = SEQ // tq
    return pl.pallas_call(
        _xattn_prompt_kernel,
        out_shape=jax.ShapeDtypeStruct((T, D_MODEL), BF16),
        grid=(BATCH, per_seq),
        in_specs=[pl.BlockSpec(qblk, lambda b, s: (b * per_seq + s, 0)),
                  pl.BlockSpec(kblk, lambda b, s: (b, 0)),
                  pl.BlockSpec(kblk, lambda b, s: (b, 0))],
        out_specs=pl.BlockSpec(qblk, lambda b, s: (b * per_seq + s, 0)),
        compiler_params=_params(("parallel", "parallel"),
                                [(qblk, BF16)] * 2 + [(kblk, BF16)] * 2,
                                temps=4 * _nbytes((tq, MEM_LEN), F32) + _nbytes(qblk, F32)),
        name="xattn_prompt",
    )(q, k16, v16)


def _xattn_sample_kernel(q_ref, k_ref, v_ref, o_in, o_ref):
    del o_in
    scale = X_HEAD_DIM ** -0.5
    q = q_ref[...]
    owner = lax.broadcasted_iota(jnp.int32, (SBLK, MEM_LEN), 0) % SB
    acc = jnp.zeros((SBLK, X_HEAD_DIM), F32)
    for i in range(SB):
        s = lax.dot_general(q, k_ref[i].astype(BF16), _NT, preferred_element_type=F32) * scale
        pr = jnp.where(owner == i, _softmax_rows(s), 0.0).astype(BF16)
        acc = acc + _dot(pr, v_ref[i].astype(BF16))
    o_ref[...] = acc.astype(o_ref.dtype)


def _xattn_sample_call(q, mk, mv, layer, o):
    qblk = (SBLK, X_HEAD_DIM)
    kblk = (None, SB, MEM_LEN, X_HEAD_DIM)
    blk0 = T_P // SBLK
    return pl.pallas_call(
        _xattn_sample_kernel,
        out_shape=jax.ShapeDtypeStruct((T, D_MODEL), BF16),
        grid=(N_SBLK, X_HEADS),
        in_specs=[pl.BlockSpec(qblk, lambda j, h: (blk0 + j, h)),
                  pl.BlockSpec(kblk, lambda j, h: (layer, j, 0, h)),
                  pl.BlockSpec(kblk, lambda j, h: (layer, j, 0, h)),
                  pl.BlockSpec(memory_space=pl.ANY)],
        out_specs=pl.BlockSpec(qblk, lambda j, h: (blk0 + j, h)),
        input_output_aliases={3: 0},
        compiler_params=_params(("parallel", "parallel"),
                                [(qblk, BF16)] * 2 + [((SB, MEM_LEN, X_HEAD_DIM), F32)] * 2,
                                temps=8 * _nbytes((MEM_LEN, X_HEAD_DIM), F32)),
        name="xattn_sample",
    )(q, mk, mv, o)


def _resid_kernel(a_ref, w_ref, x_ref, gn_ref, xo_ref, ho_ref):
    xn = x_ref[...] + _dot(a_ref[...], w_ref[...])
    xo_ref[...] = xn
    ho_ref[...] = _rms(xn, gn_ref[...]).astype(ho_ref.dtype)


def _resid_call(a, w, x, gn, tm=512):
    blk = (tm, D_MODEL)
    row_map = lambda i: (i, 0)
    return pl.pallas_call(
        _resid_kernel,
        out_shape=(jax.ShapeDtypeStruct((T, D_MODEL), F32),
                   jax.ShapeDtypeStruct((T, D_MODEL), BF16)),
        grid=(T // tm,),
        in_specs=[pl.BlockSpec(blk, row_map),
                  _resident((D_MODEL, D_MODEL), lambda i: (0, 0)),
                  pl.BlockSpec(blk, row_map),
                  pl.BlockSpec((1, D_MODEL), lambda i: (0, 0))],
        out_specs=(pl.BlockSpec(blk, row_map), pl.BlockSpec(blk, row_map)),
        compiler_params=_params(("parallel",),
                                [(blk, BF16)] * 2 + [(blk, F32)] * 2,
                                resident=_nbytes((D_MODEL, D_MODEL), BF16),
                                temps=2 * _nbytes(blk, F32)),
        name="attn_out",
    )(a, w, x, gn)


def _mlp_kernel(h_ref, wu_ref, wd_ref, x_ref, gn_ref, xo_ref, ho_ref, acc):
    f = pl.program_id(1)

    @pl.when(f == 0)
    def _():
        acc[...] = x_ref[...]

    hid = jnp.maximum(_dot(h_ref[...], wu_ref[...]), 0.0)
    acc[...] += _dot((hid * hid).astype(BF16), wd_ref[...])

    @pl.when(f == pl.num_programs(1) - 1)
    def _():
        xn = acc[...]
        xo_ref[...] = xn
        ho_ref[...] = _rms(xn, gn_ref[...]).astype(ho_ref.dtype)


def _mlp_call(h, wu, wd, x, gn, out_dtype, tm=512, tf=1024):
    blk = (tm, D_MODEL)
    row_map = lambda i, f: (i, 0)
    return pl.pallas_call(
        _mlp_kernel,
        out_shape=(jax.ShapeDtypeStruct((T, D_MODEL), F32),
                   jax.ShapeDtypeStruct((T, D_MODEL), out_dtype)),
        grid=(T // tm, D_FF // tf),
        in_specs=[pl.BlockSpec(blk, row_map),
                  pl.BlockSpec((D_MODEL, tf), lambda i, f: (0, f)),
                  pl.BlockSpec((tf, D_MODEL), lambda i, f: (f, 0)),
                  pl.BlockSpec(blk, row_map),
                  pl.BlockSpec((1, D_MODEL), lambda i, f: (0, 0))],
        out_specs=(pl.BlockSpec(blk, row_map), pl.BlockSpec(blk, row_map)),
        scratch_shapes=[pltpu.VMEM(blk, F32)],
        compiler_params=_params(
            ("parallel", "arbitrary"),
            [(blk, BF16), ((D_MODEL, tf), BF16), ((tf, D_MODEL), BF16), (blk, F32), (blk, F32),
             (blk, out_dtype)],
            resident=_nbytes(blk, F32), temps=2 * _nbytes((tm, tf), F32) + _nbytes(blk, F32)),
        name="mlp",
    )(h, wu, wd, x, gn)


def _sample_to_rows(a):
    c = a.shape[-1]
    return a.reshape(N_SBLK, SB, DEC_SEQ, c).transpose(0, 2, 1, 3).reshape(T_S, c)


def _rows_to_sample(a):
    c = a.shape[-1]
    return a.reshape(N_SBLK, DEC_SEQ, SB, c).transpose(0, 2, 1, 3).reshape(DEC_BATCH, DEC_SEQ, c)


def kernel(x_prompt, x_sample, mem_prompt, cache_mem_k, cache_mem_v, state_conv, state_pool,
           g_mix, w_in, ln_v_g, ln_v_b, w_s, b_s, conv_w, conv_b, ln_c_g, ln_c_b,
           pool_w, pool_scale, w_branch, w_out, g_xattn, g_mem, w_xq, w_xk, w_xv, w_xo,
           g_mlp, w_up, w_down, g_final):
    row = lambda a: a.reshape(1, -1)
    x = jnp.concatenate([x_prompt.reshape(T_P, D_MODEL), _sample_to_rows(x_sample)], axis=0)
    mem = mem_prompt.reshape(BATCH * MEM_LEN, D_MODEL)
    cache_k = cache_mem_k.reshape(DEPTH, DEC_BATCH, MEM_LEN, D_MODEL)
    cache_v = cache_mem_v.reshape(DEPTH, DEC_BATCH, MEM_LEN, D_MODEL)
    n1 = D_BR // 1024

    h = _norm_call(x, row(g_mix[0]))
    mk_p, mv_p, conv_p, pool_p, conv_s, pool_s, chunk_s = [], [], [], [], [], [], []
    y = None
    for l in range(DEPTH):
        w_in_l = w_in[l].astype(BF16)
        ug = _proj_call(h, w_in_l, col0=0, ncols=n1, mode="gelu", out_dtype=F32, name="proj_u")
        vn = _proj_call(h, w_in_l, col0=n1, ncols=n1, mode="gelu_ln", out_dtype=F32,
                        name="proj_v", extra=(row(ln_v_g[l]), row(ln_v_b[l])))
        glu = _proj_call(h, w_in_l, col0=2 * n1, ncols=n1, mode="glu", out_dtype=F32,
                         name="proj_glu")
        p = _proj_call(h, w_in_l, col0=4 * n1, ncols=n1, mode="id", out_dtype=F32, name="proj_p")
        gates = _proj_call(h, w_in_l, col0=5 * n1, ncols=N_BRANCH * D_MODEL // 1024,
                           mode="sigmoid", out_dtype=F32, name="proj_gate")

        cw, cb = conv_w[l], row(conv_b[l])
        lg, lb = row(ln_c_g[l]), row(ln_c_b[l])
        pw, psc = pool_w[l].astype(BF16), row(pool_scale[l])
        bs_full = jnp.repeat(b_s[l].T, DH_A, axis=1)
        ya, yb, yc = _mix_prompt_call(ug, vn, glu, p, w_s[l], bs_full, cw, cb, lg, lb, pw, psc)
        wsm = jnp.repeat(w_s[l][:, :DEC_SEQ, :DEC_SEQ].transpose(1, 2, 0), DH_A, axis=2)
        bsm = jnp.repeat(b_s[l][:, :DEC_SEQ].T, DH_A, axis=1)
        cpast = state_conv[l].transpose(1, 0, 2)
        ppast = state_pool[l].transpose(1, 0, 2)
        ya, yb, yc = _mix_sample_call(ug, vn, glu, p, cpast, ppast, wsm, bsm, cw, cb, lg, lb,
                                      pw, psc, ya, yb, yc)

        x, hx = _merge_call(ya, yb, yc, gates, x, w_branch[l].astype(BF16),
                            w_out[l].astype(BF16), row(g_xattn[l]))

        k32, v32, k16, v16 = _memkv_call(mem, row(g_mem[l]), w_xk[l].astype(BF16),
                                         w_xv[l].astype(BF16))
        q = _proj_call(hx, w_xq[l].astype(BF16), col0=0, ncols=D_MODEL // 1024, mode="id",
                       out_dtype=BF16, name="proj_q")
        o = _xattn_prompt_call(q, k16, v16)
        o = _xattn_sample_call(q, cache_k, cache_v, l, o)
        x, hm = _resid_call(o, w_xo[l].astype(BF16), x, row(g_mlp[l]))

        last = l == DEPTH - 1
        gn = g_final if last else g_mix[l + 1]
        x, h = _mlp_call(hm, w_up[l].astype(BF16), w_down[l].astype(BF16), x, row(gn),
                         F32 if last else BF16)
        y = h

        mk_p.append(k32.reshape(BATCH, MEM_LEN, X_HEADS, X_HEAD_DIM))
        mv_p.append(v32.reshape(BATCH, MEM_LEN, X_HEADS, X_HEAD_DIM))
        conv_p.append(glu[:T_P].reshape(BATCH, SEQ, D_CONV)[:, SEQ - (CONV_K - 1):])
        pool_p.append(p[:T_P].reshape(BATCH, SEQ, D_POOL)[:, SEQ - POOL_PAST:])
        conv_s.append(jnp.concatenate([state_conv[l][:, DEC_SEQ:], _rows_to_sample(glu[T_P:])],
                                      axis=1))
        pool_s.append(jnp.concatenate([state_pool[l][:, DEC_SEQ:], _rows_to_sample(p[T_P:])],
                                      axis=1))
        chunk_s.append(_rows_to_sample(vn[T_P:]))

    y_prompt = y[:T_P].reshape(BATCH, SEQ, D_MODEL)
    y_sample = _rows_to_sample(y[T_P:])
    return (y_prompt, y_sample, jnp.stack(mk_p), jnp.stack(mv_p), jnp.stack(conv_p),
            jnp.stack(pool_p), jnp.stack(conv_s), jnp.stack(pool_s), jnp.stack(chunk_s))
```

```python
import functools

import jax
import jax.numpy as jnp
from jax import lax
from jax.experimental import pallas as pl
from jax.experimental.pallas import tpu as pltpu

D_MODEL = 2048
BATCH = 4
SEQ = 2048
DEPTH = 2
DEC_BATCH = 128
DEC_SEQ = 4
PAST_LEN = 16384
D_BR = D_MODEL // 2
D_A = D_BR
H_A = 4
DH_A = D_A // H_A
CHUNK = 128
D_CONV = D_BR
CONV_K = 31
D_POOL = D_BR
POOL_WINDOWS = (2, 4, 8, 16)
N_POOL = len(POOL_WINDOWS)
G_POOL = D_POOL // N_POOL
POOL_PAST = max(POOL_WINDOWS) - 1
MEM_LEN = 256
X_HEADS = 4
X_HEAD_DIM = D_MODEL // X_HEADS
D_FF = 4 * D_MODEL
N_BRANCH = 3
D_IN = 2 * D_A + 2 * D_CONV + D_POOL + N_BRANCH * D_MODEL
RMS_EPS = 1e-6
LN_EPS = 1e-5

T_P = BATCH * SEQ
T_S = DEC_BATCH * DEC_SEQ
T = T_P + T_S
SB = 16
SBLK = DEC_SEQ * SB
N_SBLK = DEC_BATCH // SB

SUBLANES = 8
VMEM_PHYSICAL_BYTES = 64 * 1024 * 1024
VMEM_CEILING_BYTES = VMEM_PHYSICAL_BYTES - 6 * 1024 * 1024

F32 = jnp.float32
BF16 = jnp.bfloat16


def _nbytes(shape, dtype):
    n = 1
    for s in shape:
        n *= s
    return n * jnp.dtype(dtype).itemsize


def _params(semantics, pipelined, resident=0, temps=0):
    need = 2 * sum(_nbytes(s, d) for s, d in pipelined) + resident + temps
    need = need + need // 8 + (2 << 20)
    return pltpu.CompilerParams(
        dimension_semantics=semantics,
        vmem_limit_bytes=int(min(max(need, 16 << 20), VMEM_CEILING_BYTES)))


def _resident(shape, index_map):
    return pl.BlockSpec(shape, index_map, pipeline_mode=pl.Buffered(1))


def _sigmoid(x):
    return 1.0 / (1.0 + jnp.exp(-x))


def _rms(x, g):
    return x * lax.rsqrt(jnp.mean(x * x, axis=-1, keepdims=True) + RMS_EPS) * g


def _ln(x, g, b):
    mu = jnp.mean(x, axis=-1, keepdims=True)
    xc = x - mu
    var = jnp.mean(xc * xc, axis=-1, keepdims=True)
    return xc * lax.rsqrt(var + LN_EPS) * g + b


def _dot(a, b):
    return jnp.dot(a, b, preferred_element_type=F32)


def _norm_kernel(n_prompt, xp_ref, xs_ref, g_ref, x_ref, h_ref):
    def emit(src):
        x = src[...]
        x_ref[...] = x
        h_ref[...] = _rms(x, g_ref[...]).astype(h_ref.dtype)

    @pl.when(pl.program_id(0) < n_prompt)
    def _():
        emit(xp_ref)

    @pl.when(pl.program_id(0) >= n_prompt)
    def _():
        emit(xs_ref)


def _norm_call(xp, xs, g, tm=512):
    n_prompt = T_P // tm
    blk = (tm, D_MODEL)
    return pl.pallas_call(
        functools.partial(_norm_kernel, n_prompt),
        out_shape=(jax.ShapeDtypeStruct((T, D_MODEL), F32),
                   jax.ShapeDtypeStruct((T, D_MODEL), BF16)),
        grid=(T // tm,),
        in_specs=[pl.BlockSpec(blk, lambda i: (jnp.minimum(i, n_prompt - 1), 0)),
                  pl.BlockSpec(blk, lambda i: (jnp.maximum(i - n_prompt, 0), 0)),
                  pl.BlockSpec((1, D_MODEL), lambda i: (0, 0))],
        out_specs=(pl.BlockSpec(blk, lambda i: (i, 0)), pl.BlockSpec(blk, lambda i: (i, 0))),
        compiler_params=_params(("arbitrary",), [(blk, F32)] * 3 + [(blk, BF16)],
                                temps=2 * _nbytes(blk, F32)),
        name="rmsnorm_in",
    )(xp, xs, g)


def _proj_kernel(mode, *refs):
    n_w = 2 if mode == "glu" else 1
    n_extra = 2 if mode == "gelu_ln" else 0
    h_ref, w_refs = refs[0], refs[1:1 + n_w]
    extra = refs[1 + n_w:1 + n_w + n_extra]
    o_ref = refs[1 + n_w + n_extra]
    wbf = refs[2 + n_w + n_extra:]

    @pl.when(pl.program_id(1) == 0)
    def _():
        for src, dst in zip(w_refs, wbf):
            dst[...] = src[...].astype(BF16)

    h = h_ref[...]
    z = _dot(h, wbf[0][...])
    if mode == "glu":
        z = z * _sigmoid(_dot(h, wbf[1][...]))
    elif mode == "gelu":
        z = jax.nn.gelu(z)
    elif mode == "gelu_ln":
        z = _ln(jax.nn.gelu(z), extra[0][...], extra[1][...])
    elif mode == "sigmoid":
        z = _sigmoid(z)
    o_ref[...] = z.astype(o_ref.dtype)


def _proj_call(h, w, layer, *, col0, ncols, mode, out_dtype, name, extra=(), tm=1088, tn=1024):
    n, k = h.shape
    hblk, wblk, oblk = (tm, k), (k, tn), (tm, tn)
    n_w = 2 if mode == "glu" else 1
    in_specs = [pl.BlockSpec(hblk, lambda j, i: (i, 0)),
                pl.BlockSpec((None,) + wblk, lambda j, i: (layer, 0, col0 + j))]
    args = [h, w]
    if mode == "glu":
        in_specs.append(pl.BlockSpec((None,) + wblk, lambda j, i: (layer, 0, col0 + ncols + j)))
        args.append(w)
    for e in extra:
        in_specs.append(pl.BlockSpec((1, tn), lambda j, i: (0, 0)))
        args.append(e)
    return pl.pallas_call(
        functools.partial(_proj_kernel, mode),
        out_shape=jax.ShapeDtypeStruct((n, ncols * tn), out_dtype),
        grid=(ncols, n // tm),
        in_specs=in_specs,
        out_specs=pl.BlockSpec(oblk, lambda j, i: (i, j)),
        scratch_shapes=[pltpu.VMEM(wblk, BF16)] * n_w,
        compiler_params=_params(("arbitrary", "arbitrary"),
                                [(hblk, h.dtype), (oblk, out_dtype)] + [(wblk, F32)] * n_w,
                                resident=n_w * _nbytes(wblk, BF16),
                                temps=(2 + n_w) * _nbytes(oblk, F32)),
        name=name,
    )(*args)


MIX_TM = 256
MIX_RB = 32
CONV_HALO = 32
POOL_HALO = 16


def _conv_ln_silu(window, cw_ref, cb, lg, lb):
    acc = cb + cw_ref[0:1, :] * window(0)
    for k in range(1, CONV_K):
        acc = acc + cw_ref[k:k + 1, :] * window(k)
    y = _ln(acc, lg, lb)
    return y * _sigmoid(y)


def _mix_prompt_kernel(ug_ref, vn_ref, glu_ref, gh_ref, p_ref, ph_ref,
                       ws_ref, bs_ref, cw_ref, cb_ref, lg_ref, lb_ref, pw_ref, psc_ref,
                       ya_ref, yb_ref, yc_ref, gwin, gshift, pwin):
    tm = MIX_TM
    seq_tile = pl.program_id(0) % (SEQ // tm)
    first = seq_tile == 0

    row = lax.broadcasted_iota(jnp.int32, (CHUNK, CHUNK), 0)
    col = lax.broadcasted_iota(jnp.int32, (CHUNK, CHUNK), 1)
    for h in range(H_A):
        w = jnp.where(row >= col, ws_ref[h], 0.0).astype(BF16)
        hs = slice(h * DH_A, (h + 1) * DH_A)
        for c in range(tm // CHUNK):
            rs = slice(c * CHUNK, (c + 1) * CHUNK)
            mixed = _dot(w, vn_ref[rs, hs].astype(BF16)) + bs_ref[:, hs]
            ya_ref[rs, hs] = (ug_ref[rs, hs] * mixed).astype(ya_ref.dtype)

    gwin[0:CONV_HALO, :] = jnp.where(first, 0.0, gh_ref[...])
    gwin[CONV_HALO:, :] = glu_ref[...]
    n_shift = CONV_HALO + tm - SUBLANES
    for r in range(1, SUBLANES):
        gshift[r - 1, 0:n_shift, :] = gwin[r:r + n_shift, :]
    cb, lg, lb = cb_ref[...], lg_ref[...], lb_ref[...]
    off = CONV_HALO - (CONV_K - 1)

    def window(r0, k):
        a, r = divmod(off + k, SUBLANES)
        lo = r0 + SUBLANES * a
        if r == 0:
            return gwin[lo:lo + MIX_RB, :]
        return gshift[r - 1, lo:lo + MIX_RB, :]

    for rb in range(tm // MIX_RB):
        r0 = rb * MIX_RB
        y = _conv_ln_silu(functools.partial(window, r0), cw_ref, cb, lg, lb)
        yb_ref[r0:r0 + MIX_RB, :] = y.astype(yb_ref.dtype)

    pwin[0:POOL_HALO, :] = jnp.where(first, 0.0, ph_ref[...])
    pwin[POOL_HALO:, :] = p_ref[...]
    pos = seq_tile * tm + lax.broadcasted_iota(jnp.int32, (tm, 1), 0)
    for g, wlen in enumerate(POOL_WINDOWS):
        gs = slice(g * G_POOL, (g + 1) * G_POOL)
        s = pwin[POOL_HALO:POOL_HALO + tm, gs]
        for d in range(1, wlen):
            s = s + pwin[POOL_HALO - d:POOL_HALO - d + tm, gs]
        inv = 1.0 / jnp.minimum(wlen, pos + 1).astype(F32)
        mixed = s * inv - p_ref[:, gs]
        y = _dot(mixed.astype(BF16), pw_ref[g]) * psc_ref[:, gs]
        yc_ref[:, gs] = y.astype(yc_ref.dtype)


def _mix_prompt_call(ug, vn, glu, p, ws, bs_full, cw, cb, lg, lb, pw, psc):
    tm = MIX_TM
    blk = (tm, D_BR)
    row_map = lambda i: (i, 0)
    const2 = lambda i: (0, 0)
    const3 = lambda i: (0, 0, 0)
    in_specs = [
        pl.BlockSpec(blk, row_map),
        pl.BlockSpec(blk, row_map),
        pl.BlockSpec(blk, row_map),
        pl.BlockSpec((CONV_HALO, D_BR),
                     lambda i: (jnp.maximum(i * (tm // CONV_HALO) - 1, 0), 0)),
        pl.BlockSpec(blk, row_map),
        pl.BlockSpec((POOL_HALO, D_BR),
                     lambda i: (jnp.maximum(i * (tm // POOL_HALO) - 1, 0), 0)),
        pl.BlockSpec((H_A, CHUNK, CHUNK), const3),
        pl.BlockSpec((CHUNK, D_A), const2),
        pl.BlockSpec((CONV_K, D_CONV), const2),
        pl.BlockSpec((1, D_CONV), const2),
        pl.BlockSpec((1, D_CONV), const2),
        pl.BlockSpec((1, D_CONV), const2),
        pl.BlockSpec((N_POOL, G_POOL, G_POOL), const3),
        pl.BlockSpec((1, D_POOL), const2),
    ]
    out = jax.ShapeDtypeStruct((T, D_BR), BF16)
    return pl.pallas_call(
        _mix_prompt_kernel,
        out_shape=(out, out, out),
        grid=(T_P // tm,),
        in_specs=in_specs,
        out_specs=(pl.BlockSpec(blk, row_map),) * 3,
        scratch_shapes=[pltpu.VMEM((CONV_HALO + tm, D_CONV), F32),
                        pltpu.VMEM((SUBLANES - 1, CONV_HALO + tm, D_CONV), F32),
                        pltpu.VMEM((POOL_HALO + tm, D_POOL), F32)],
        compiler_params=_params(
            ("parallel",),
            [(blk, F32)] * 4 + [(blk, BF16)] * 3 + [((CONV_HALO + POOL_HALO, D_BR), F32),
                                                    ((CHUNK + CONV_K + 8, D_BR), F32)],
            resident=_nbytes((SUBLANES * (CONV_HALO + tm) + POOL_HALO + tm, D_BR), F32),
            temps=20 * _nbytes(blk, F32)),
        name="mix_prompt",
    )(ug, vn, glu, glu, p, p, ws, bs_full, cw, cb, lg, lb, pw, psc)


def _mix_sample_kernel(ug_ref, vn_ref, glu_ref, p_ref, cpast_ref, ppast_ref,
                       wsm_ref, bsm_ref, cw_ref, cb_ref, lg_ref, lb_ref, pw_ref, psc_ref,
                       ya_in, yb_in, yc_in, ya_ref, yb_ref, yc_ref, mix_scr):
    del ya_in, yb_in, yc_in
    cb, lg, lb = cb_ref[...], lg_ref[...], lb_ref[...]

    def rows(t):
        return slice(t * SB, (t + 1) * SB)

    def conv_src(j):
        return cpast_ref[j] if j < CONV_K - 1 else glu_ref[rows(j - (CONV_K - 1)), :]

    def pool_src(j, gs):
        return ppast_ref[j, :, gs] if j < POOL_PAST else p_ref[rows(j - POOL_PAST), gs]

    for t in range(DEC_SEQ):
        m = bsm_ref[t:t + 1, :] + wsm_ref[t, 0:1, :] * vn_ref[rows(0), :]
        for s in range(1, t + 1):
            m = m + wsm_ref[t, s:s + 1, :] * vn_ref[rows(s), :]
        ya_ref[rows(t), :] = (ug_ref[rows(t), :] * m).astype(ya_ref.dtype)
        y = _conv_ln_silu(lambda k: conv_src(t + k), cw_ref, cb, lg, lb)
        yb_ref[rows(t), :] = y.astype(yb_ref.dtype)
        for g, wlen in enumerate(POOL_WINDOWS):
            gs = slice(g * G_POOL, (g + 1) * G_POOL)
            s = pool_src(POOL_PAST + t, gs)
            for d in range(1, wlen):
                s = s + pool_src(POOL_PAST + t - d, gs)
            cnt = float(min(wlen, PAST_LEN + t + 1))
            mix_scr[rows(t), gs] = s / cnt - p_ref[rows(t), gs]
    for g in range(N_POOL):
        gs = slice(g * G_POOL, (g + 1) * G_POOL)
        y = _dot(mix_scr[:, gs].astype(BF16), pw_ref[g]) * psc_ref[:, gs]
        yc_ref[:, gs] = y.astype(yc_ref.dtype)


def _mix_sample_call(ug, vn, glu, p, cpast, ppast, wsm, bsm, cw, cb, lg, lb, pw, psc, ya, yb, yc):
    blk = (SBLK, D_BR)
    blk0 = T_P // SBLK
    row_map = lambda j: (blk0 + j, 0)
    const2 = lambda j: (0, 0)
    const3 = lambda j: (0, 0, 0)
    any_spec = pl.BlockSpec(memory_space=pl.ANY)
    in_specs = [
        pl.BlockSpec(blk, row_map), pl.BlockSpec(blk, row_map),
        pl.BlockSpec(blk, row_map), pl.BlockSpec(blk, row_map),
        pl.BlockSpec((CONV_K - 1, SB, D_CONV), lambda j: (0, j, 0)),
        pl.BlockSpec((POOL_PAST, SB, D_POOL), lambda j: (0, j, 0)),
        pl.BlockSpec((DEC_SEQ, DEC_SEQ, D_A), const3),
        pl.BlockSpec((DEC_SEQ, D_A), const2),
        pl.BlockSpec((CONV_K, D_CONV), const2),
        pl.BlockSpec((1, D_CONV), const2),
        pl.BlockSpec((1, D_CONV), const2),
        pl.BlockSpec((1, D_CONV), const2),
        pl.BlockSpec((N_POOL, G_POOL, G_POOL), const3),
        pl.BlockSpec((1, D_POOL), const2),
        any_spec, any_spec, any_spec,
    ]
    out = jax.ShapeDtypeStruct((T, D_BR), BF16)
    return pl.pallas_call(
        _mix_sample_kernel,
        out_shape=(out, out, out),
        grid=(N_SBLK,),
        in_specs=in_specs,
        out_specs=(pl.BlockSpec(blk, row_map),) * 3,
        scratch_shapes=[pltpu.VMEM(blk, F32)],
        input_output_aliases={14: 0, 15: 1, 16: 2},
        compiler_params=_params(
            ("parallel",),
            [(blk, F32)] * 4 + [(blk, BF16)] * 3
            + [((CONV_K - 1 + POOL_PAST, SB, D_BR), F32), ((CONV_K + 16, D_BR), F32)],
            resident=_nbytes(blk, F32), temps=32 * _nbytes(blk, F32)),
        name="mix_sample",
    )(ug, vn, glu, p, cpast, ppast, wsm, bsm, cw, cb, lg, lb, pw, psc, ya, yb, yc)


def _merge_kernel(ya_ref, yb_ref, yc_ref, g0_ref, g1_ref, g2_ref, x_ref,
                  wb_ref, wo_ref, gn_ref, xo_ref, ho_ref):
    merged = g0_ref[...] * _dot(ya_ref[...], wb_ref[0])
    merged = merged + g1_ref[...] * _dot(yb_ref[...], wb_ref[1])
    merged = merged + g2_ref[...] * _dot(yc_ref[...], wb_ref[2])
    xn = x_ref[...] + _dot(merged.astype(BF16), wo_ref[...])
    xo_ref[...] = xn
    ho_ref[...] = _rms(xn, gn_ref[...]).astype(ho_ref.dtype)


def _merge_call(ya, yb, yc, gates, x, wb, wo, layer, gn, tm=256):
    yblk, xblk = (tm, D_BR), (tm, D_MODEL)
    row_map = lambda i: (i, 0)
    in_specs = [
        pl.BlockSpec(yblk, row_map), pl.BlockSpec(yblk, row_map), pl.BlockSpec(yblk, row_map),
        pl.BlockSpec(xblk, lambda i: (i, 0)),
        pl.BlockSpec(xblk, lambda i: (i, 1)),
        pl.BlockSpec(xblk, lambda i: (i, 2)),
        pl.BlockSpec(xblk, row_map),
        _resident((None, N_BRANCH, D_BR, D_MODEL), lambda i: (layer, 0, 0, 0)),
        _resident((None, D_MODEL, D_MODEL), lambda i: (layer, 0, 0)),
        pl.BlockSpec((1, D_MODEL), lambda i: (0, 0)),
    ]
    return pl.pallas_call(
        _merge_kernel,
        out_shape=(jax.ShapeDtypeStruct((T, D_MODEL), F32),
                   jax.ShapeDtypeStruct((T, D_MODEL), BF16)),
        grid=(T // tm,),
        in_specs=in_specs,
        out_specs=(pl.BlockSpec(xblk, row_map), pl.BlockSpec(xblk, row_map)),
        compiler_params=_params(
            ("parallel",),
            [(yblk, BF16)] * 3 + [(xblk, gates.dtype)] * 3 + [(xblk, F32)] * 2 + [(xblk, BF16)],
            resident=_nbytes((N_BRANCH, D_BR, D_MODEL), BF16) + _nbytes((D_MODEL, D_MODEL), BF16),
            temps=3 * _nbytes(xblk, F32)),
        name="merge_out",
    )(ya, yb, yc, gates, gates, gates, x, wb, wo, gn)


def _memkv_kernel(m_ref, g_ref, wk_ref, wv_ref, k32_ref, v32_ref, k16_ref, v16_ref):
    m = _rms(m_ref[...], g_ref[...]).astype(BF16)
    k = _dot(m, wk_ref[...])
    v = _dot(m, wv_ref[...])
    k32_ref[...] = k
    v32_ref[...] = v
    k16_ref[...] = k.astype(BF16)
    v16_ref[...] = v.astype(BF16)


def _memkv_call(mem, g, wk, wv, layer, tn=1024):
    n = mem.shape[0]
    mblk, wblk, oblk = (n, D_MODEL), (D_MODEL, tn), (n, tn)
    o32 = jax.ShapeDtypeStruct((n, D_MODEL), F32)
    o16 = jax.ShapeDtypeStruct((n, D_MODEL), BF16)
    col = lambda j: (0, j)
    wcol = lambda j: (layer, 0, j)
    return pl.pallas_call(
        _memkv_kernel,
        out_shape=(o32, o32, o16, o16),
        grid=(D_MODEL // tn,),
        in_specs=[_resident(mblk, lambda j: (0, 0)),
                  pl.BlockSpec((1, D_MODEL), lambda j: (0, 0)),
                  pl.BlockSpec((None,) + wblk, wcol), pl.BlockSpec((None,) + wblk, wcol)],
        out_specs=(pl.BlockSpec(oblk, col),) * 4,
        compiler_params=_params(
            ("parallel",),
            [(wblk, BF16)] * 2 + [(oblk, F32)] * 2 + [(oblk, BF16)] * 2,
            resident=_nbytes(mblk, F32), temps=2 * _nbytes(mblk, F32)),
        name="mem_kv",
    )(mem, g, wk, wv)


_NT = (((1,), (1,)), ((), ()))


def _softmax_rows(s):
    e = jnp.exp(s - jnp.max(s, axis=-1, keepdims=True))
    return e * (1.0 / jnp.sum(e, axis=-1, keepdims=True))


def _xattn_prompt_kernel(q_ref, k_ref, v_ref, o_ref):
    scale = X_HEAD_DIM ** -0.5
    for h in range(X_HEADS):
        hs = slice(h * X_HEAD_DIM, (h + 1) * X_HEAD_DIM)
        s = lax.dot_general(q_ref[:, hs], k_ref[:, hs], _NT, preferred_element_type=F32) * scale
        pr = _softmax_rows(s).astype(BF16)
        o_ref[:, hs] = _dot(pr, v_ref[:, hs]).astype(o_ref.dtype)


def _xattn_prompt_call(q, k16, v16, tq=512):
    qblk, kblk = (tq, D_MODEL), (MEM_LEN, D_MODEL)
    per_seq = SEQ // tq
    return pl.pallas_call(
        _xattn_prompt_kernel,
        out_shape=jax.ShapeDtypeStruct((T, D_MODEL), BF16),
        grid=(BATCH, per_seq),
        in_specs=[pl.BlockSpec(qblk, lambda b, s: (b * per_seq + s, 0)),
                  pl.BlockSpec(kblk, lambda b, s: (b, 0)),
                  pl.BlockSpec(kblk, lambda b, s: (b, 0))],
        out_specs=pl.BlockSpec(qblk, lambda b, s: (b * per_seq + s, 0)),
        compiler_params=_params(("parallel", "parallel"),
                                [(qblk, BF16)] * 2 + [(kblk, BF16)] * 2,
                                temps=4 * _nbytes((tq, MEM_LEN), F32) + _nbytes(qblk, F32)),
        name="xattn_prompt",
    )(q, k16, v16)


LANES = 128
HEAD_LANE_TILES = X_HEAD_DIM // LANES
KV_ROWS = MEM_LEN * HEAD_LANE_TILES * X_HEADS
KV_BB = 4


def _cache_rows(c):
    d, b = c.shape[:2]
    c = c.reshape(d, b, MEM_LEN, X_HEADS, HEAD_LANE_TILES, LANES).transpose(0, 1, 2, 4, 3, 5)
    return c.reshape(d, b, KV_ROWS, LANES)


def _head_rows(ref, i, h):
    stride = HEAD_LANE_TILES * X_HEADS
    tiles = [ref[i, pl.ds(t * X_HEADS + h, MEM_LEN, stride=stride), :]
             for t in range(HEAD_LANE_TILES)]
    return jnp.concatenate(tiles, axis=1)


def _xattn_sample_kernel(q_ref, k_ref, v_ref, o_in, o_ref, acc):
    del o_in
    c = pl.program_id(1)
    scale = X_HEAD_DIM ** -0.5

    @pl.when(c == 0)
    def _():
        acc[...] = jnp.zeros_like(acc)

    owner = lax.broadcasted_iota(jnp.int32, (SBLK, MEM_LEN), 0) % SB
    for ii in range(KV_BB):
        mine = owner == c * KV_BB + ii
        for h in range(X_HEADS):
            hs = slice(h * X_HEAD_DIM, (h + 1) * X_HEAD_DIM)
            kh = _head_rows(k_ref, ii, h).astype(BF16)
            s = lax.dot_general(q_ref[:, hs], kh, _NT, preferred_element_type=F32) * scale
            pr = jnp.where(mine, _softmax_rows(s), 0.0).astype(BF16)
            acc[:, hs] += _dot(pr, _head_rows(v_ref, ii, h).astype(BF16))

    @pl.when(c == pl.num_programs(1) - 1)
    def _():
        o_ref[...] = acc[...].astype(o_ref.dtype)


def _xattn_sample_call(q, mk, mv, layer, o):
    qblk = (SBLK, D_MODEL)
    kblk = (None, KV_BB, KV_ROWS, LANES)
    blk0 = T_P // SBLK
    per_blk = SB // KV_BB
    return pl.pallas_call(
        _xattn_sample_kernel,
        out_shape=jax.ShapeDtypeStruct((T, D_MODEL), BF16),
        grid=(N_SBLK, per_blk),
        in_specs=[pl.BlockSpec(qblk, lambda j, c: (blk0 + j, 0)),
                  pl.BlockSpec(kblk, lambda j, c: (layer, j * per_blk + c, 0, 0)),
                  pl.BlockSpec(kblk, lambda j, c: (layer, j * per_blk + c, 0, 0)),
                  pl.BlockSpec(memory_space=pl.ANY)],
        out_specs=pl.BlockSpec(qblk, lambda j, c: (blk0 + j, 0)),
        scratch_shapes=[pltpu.VMEM(qblk, F32)],
        input_output_aliases={3: 0},
        compiler_params=_params(("parallel", "arbitrary"),
                                [(qblk, BF16)] * 2 + [((KV_BB, KV_ROWS, LANES), F32)] * 2,
                                resident=_nbytes(qblk, F32),
                                temps=8 * _nbytes((MEM_LEN, X_HEAD_DIM), F32)),
        name="xattn_sample",
    )(q, mk, mv, o)


def _resid_kernel(a_ref, w_ref, x_ref, gn_ref, xo_ref, ho_ref):
    xn = x_ref[...] + _dot(a_ref[...], w_ref[...])
    xo_ref[...] = xn
    ho_ref[...] = _rms(xn, gn_ref[...]).astype(ho_ref.dtype)


def _resid_call(a, w, layer, x, gn, tm=512):
    blk = (tm, D_MODEL)
    row_map = lambda i: (i, 0)
    return pl.pallas_call(
        _resid_kernel,
        out_shape=(jax.ShapeDtypeStruct((T, D_MODEL), F32),
                   jax.ShapeDtypeStruct((T, D_MODEL), BF16)),
        grid=(T // tm,),
        in_specs=[pl.BlockSpec(blk, row_map),
                  _resident((None, D_MODEL, D_MODEL), lambda i: (layer, 0, 0)),
                  pl.BlockSpec(blk, row_map),
                  pl.BlockSpec((1, D_MODEL), lambda i: (0, 0))],
        out_specs=(pl.BlockSpec(blk, row_map), pl.BlockSpec(blk, row_map)),
        compiler_params=_params(("parallel",),
                                [(blk, BF16)] * 2 + [(blk, F32)] * 2,
                                resident=_nbytes((D_MODEL, D_MODEL), BF16),
                                temps=2 * _nbytes(blk, F32)),
        name="attn_out",
    )(a, w, x, gn)


def _mlp_kernel(n_prompt, h_ref, wu_ref, wd_ref, x_ref, gn_ref, o0_ref, o1_ref, acc):
    i, f = pl.program_id(0), pl.program_id(1)

    @pl.when(f == 0)
    def _():
        acc[...] = x_ref[...]

    hid = jnp.maximum(_dot(h_ref[...], wu_ref[...]), 0.0)
    acc[...] += _dot((hid * hid).astype(BF16), wd_ref[...])

    @pl.when(f == pl.num_programs(1) - 1)
    def _():
        xn = acc[...]
        if n_prompt is None:
            o0_ref[...] = xn
            o1_ref[...] = _rms(xn, gn_ref[...]).astype(o1_ref.dtype)
        else:
            y = _rms(xn, gn_ref[...])

            @pl.when(i < n_prompt)
            def _():
                o0_ref[...] = y

            @pl.when(i >= n_prompt)
            def _():
                o1_ref[...] = y


def _mlp_call(h, wu, wd, layer, x, gn, final, tm=512, tf=1024):
    blk = (tm, D_MODEL)
    row_map = lambda i, f: (i, 0)
    if final:
        n_prompt = T_P // tm
        out_shape = (jax.ShapeDtypeStruct((T_P, D_MODEL), F32),
                     jax.ShapeDtypeStruct((T_S, D_MODEL), F32))
        out_specs = (pl.BlockSpec(blk, lambda i, f: (jnp.minimum(i, n_prompt - 1), 0)),
                     pl.BlockSpec(blk, lambda i, f: (jnp.maximum(i - n_prompt, 0), 0)))
        out_dtype = F32
    else:
        n_prompt = None
        out_shape = (jax.ShapeDtypeStruct((T, D_MODEL), F32),
                     jax.ShapeDtypeStruct((T, D_MODEL), BF16))
        out_specs = (pl.BlockSpec(blk, row_map), pl.BlockSpec(blk, row_map))
        out_dtype = BF16
    return pl.pallas_call(
        functools.partial(_mlp_kernel, n_prompt),
        out_shape=out_shape,
        grid=(T // tm, D_FF // tf),
        in_specs=[pl.BlockSpec(blk, row_map),
                  pl.BlockSpec((None, D_MODEL, tf), lambda i, f: (layer, 0, f)),
                  pl.BlockSpec((None, tf, D_MODEL), lambda i, f: (layer, f, 0)),
                  pl.BlockSpec(blk, row_map),
                  pl.BlockSpec((1, D_MODEL), lambda i, f: (0, 0))],
        out_specs=out_specs,
        scratch_shapes=[pltpu.VMEM(blk, F32)],
        compiler_params=_params(
            ("arbitrary", "arbitrary"),
            [(blk, BF16), ((D_MODEL, tf), BF16), ((tf, D_MODEL), BF16), (blk, F32), (blk, F32),
             (blk, out_dtype)],
            resident=_nbytes(blk, F32), temps=2 * _nbytes((tm, tf), F32) + _nbytes(blk, F32)),
        name="mlp_final" if final else "mlp",
    )(h, wu, wd, x, gn)


def _sample_to_rows(a):
    c = a.shape[-1]
    return a.reshape(N_SBLK, SB, DEC_SEQ, c).transpose(0, 2, 1, 3).reshape(T_S, c)


def _rows_to_sample(a):
    c = a.shape[-1]
    return a.reshape(N_SBLK, DEC_SEQ, SB, c).transpose(0, 2, 1, 3).reshape(DEC_BATCH, DEC_SEQ, c)


def kernel(x_prompt, x_sample, mem_prompt, cache_mem_k, cache_mem_v, state_conv, state_pool,
           g_mix, w_in, ln_v_g, ln_v_b, w_s, b_s, conv_w, conv_b, ln_c_g, ln_c_b,
           pool_w, pool_scale, w_branch, w_out, g_xattn, g_mem, w_xq, w_xk, w_xv, w_xo,
           g_mlp, w_up, w_down, g_final):
    assert T_S == 512 and T_P % 512 == 0
    row = lambda a: a.reshape(1, -1)
    mem = mem_prompt.reshape(BATCH * MEM_LEN, D_MODEL)
    cache_k, cache_v = _cache_rows(cache_mem_k), _cache_rows(cache_mem_v)
    wb16, wo16 = w_branch.astype(BF16), w_out.astype(BF16)
    wk16, wv16, wxo16 = w_xk.astype(BF16), w_xv.astype(BF16), w_xo.astype(BF16)
    wu16, wd16, pw16 = w_up.astype(BF16), w_down.astype(BF16), pool_w.astype(BF16)
    n1 = D_BR // 1024

    x, h = _norm_call(x_prompt.reshape(T_P, D_MODEL), _sample_to_rows(x_sample), row(g_mix[0]))
    mk_p, mv_p, conv_p, pool_p, conv_s, pool_s, chunk_s = [], [], [], [], [], [], []
    y_prompt = y_sample = None
    for l in range(DEPTH):
        ug = _proj_call(h, w_in, l, col0=0, ncols=n1, mode="gelu", out_dtype=F32, name="proj_u")
        vn = _proj_call(h, w_in, l, col0=n1, ncols=n1, mode="gelu_ln", out_dtype=F32,
                        name="proj_v", extra=(row(ln_v_g[l]), row(ln_v_b[l])))
        glu = _proj_call(h, w_in, l, col0=2 * D_A // 512, ncols=D_CONV // 512, mode="glu",
                         out_dtype=F32, name="proj_glu", tn=512)
        p = _proj_call(h, w_in, l, col0=4 * n1, ncols=n1, mode="id", out_dtype=F32,
                       name="proj_p")
        gates = _proj_call(h, w_in, l, col0=5 * n1, ncols=N_BRANCH * D_MODEL // 1024,
                           mode="sigmoid", out_dtype=F32, name="proj_gate")

        cw, cb = conv_w[l], row(conv_b[l])
        lg, lb = row(ln_c_g[l]), row(ln_c_b[l])
        pw, psc = pw16[l], row(pool_scale[l])
        bs_full = jnp.repeat(b_s[l].T, DH_A, axis=1)
        ya, yb, yc = _mix_prompt_call(ug, vn, glu, p, w_s[l], bs_full, cw, cb, lg, lb, pw, psc)
        wsm = jnp.repeat(w_s[l][:, :DEC_SEQ, :DEC_SEQ].transpose(1, 2, 0), DH_A, axis=2)
        bsm = jnp.repeat(b_s[l][:, :DEC_SEQ].T, DH_A, axis=1)
        cpast = state_conv[l].transpose(1, 0, 2)
        ppast = state_pool[l].transpose(1, 0, 2)
        ya, yb, yc = _mix_sample_call(ug, vn, glu, p, cpast, ppast, wsm, bsm, cw, cb, lg, lb,
                                      pw, psc, ya, yb, yc)

        x, hx = _merge_call(ya, yb, yc, gates, x, wb16, wo16, l, row(g_xattn[l]))

        k32, v32, k16, v16 = _memkv_call(mem, row(g_mem[l]), wk16, wv16, l)
        q = _proj_call(hx, w_xq, l, col0=0, ncols=D_MODEL // 1024, mode="id", out_dtype=BF16,
                       name="proj_q")
        o = _xattn_prompt_call(q, k16, v16)
        o = _xattn_sample_call(q, cache_k, cache_v, l, o)
        x, hm = _resid_call(o, wxo16, l, x, row(g_mlp[l]))

        if l == DEPTH - 1:
            y_prompt, y_sample = _mlp_call(hm, wu16, wd16, l, x, row(g_final), True)
        else:
            x, h = _mlp_call(hm, wu16, wd16, l, x, row(g_mix[l + 1]), False)

        mk_p.append(k32.reshape(BATCH, MEM_LEN, X_HEADS, X_HEAD_DIM))
        mv_p.append(v32.reshape(BATCH, MEM_LEN, X_HEADS, X_HEAD_DIM))
        conv_p.append(glu[:T_P].reshape(BATCH, SEQ, D_CONV)[:, SEQ - (CONV_K - 1):])
        pool_p.append(p[:T_P].reshape(BATCH, SEQ, D_POOL)[:, SEQ - POOL_PAST:])
        conv_s.append(jnp.concatenate([state_conv[l][:, DEC_SEQ:], _rows_to_sample(glu[T_P:])],
                                      axis=1))
        pool_s.append(jnp.concatenate([state_pool[l][:, DEC_SEQ:], _rows_to_sample(p[T_P:])],
                                      axis=1))
        chunk_s.append(_rows_to_sample(vn[T_P:]))

    y_prompt = y_prompt.reshape(BATCH, SEQ, D_MODEL)
    y_sample = _rows_to_sample(y_sample)
    return (y_prompt, y_sample, jnp.stack(mk_p), jnp.stack(mv_p), jnp.stack(conv_p),
            jnp.stack(pool_p), jnp.stack(conv_s), jnp.stack(pool_s), jnp.stack(chunk_s))
```

```python
import functools

import jax
import jax.numpy as jnp
from jax import lax
from jax.experimental import pallas as pl
from jax.experimental.pallas import tpu as pltpu

D_MODEL = 2048
BATCH = 4
SEQ = 2048
DEPTH = 2
DEC_BATCH = 128
DEC_SEQ = 4
PAST_LEN = 16384
D_BR = D_MODEL // 2
D_A = D_BR
H_A = 4
DH_A = D_A // H_A
CHUNK = 128
D_CONV = D_BR
CONV_K = 31
D_POOL = D_BR
POOL_WINDOWS = (2, 4, 8, 16)
N_POOL = len(POOL_WINDOWS)
G_POOL = D_POOL // N_POOL
POOL_PAST = max(POOL_WINDOWS) - 1
MEM_LEN = 256
X_HEADS = 4
X_HEAD_DIM = D_MODEL // X_HEADS
D_FF = 4 * D_MODEL
N_BRANCH = 3
D_IN = 2 * D_A + 2 * D_CONV + D_POOL + N_BRANCH * D_MODEL
RMS_EPS = 1e-6
LN_EPS = 1e-5

T_P = BATCH * SEQ
T_S = DEC_BATCH * DEC_SEQ
T = T_P + T_S
SB = 16
SBLK = DEC_SEQ * SB
N_SBLK = DEC_BATCH // SB

SUBLANES = 8
VMEM_PHYSICAL_BYTES = 64 * 1024 * 1024
VMEM_CEILING_BYTES = VMEM_PHYSICAL_BYTES - 6 * 1024 * 1024

F32 = jnp.float32
BF16 = jnp.bfloat16


def _nbytes(shape, dtype):
    n = 1
    for s in shape:
        n *= s
    return n * jnp.dtype(dtype).itemsize


def _params(semantics, pipelined, resident=0, temps=0):
    need = 2 * sum(_nbytes(s, d) for s, d in pipelined) + resident + temps
    need = need + need // 8 + (2 << 20)
    return pltpu.CompilerParams(
        dimension_semantics=semantics,
        vmem_limit_bytes=int(min(max(need, 16 << 20), VMEM_CEILING_BYTES)))


def _resident(shape, index_map):
    return pl.BlockSpec(shape, index_map, pipeline_mode=pl.Buffered(1))


def _sigmoid(x):
    return 1.0 / (1.0 + jnp.exp(-x))


def _rms(x, g):
    return x * lax.rsqrt(jnp.mean(x * x, axis=-1, keepdims=True) + RMS_EPS) * g


def _ln(x, g, b):
    mu = jnp.mean(x, axis=-1, keepdims=True)
    xc = x - mu
    var = jnp.mean(xc * xc, axis=-1, keepdims=True)
    return xc * lax.rsqrt(var + LN_EPS) * g + b


def _dot(a, b):
    return jnp.dot(a, b, preferred_element_type=F32)


def _norm_kernel(n_prompt, xp_ref, xs_ref, g_ref, x_ref, h_ref):
    def emit(src):
        x = src[...]
        x_ref[...] = x
        h_ref[...] = _rms(x, g_ref[...]).astype(h_ref.dtype)

    @pl.when(pl.program_id(0) < n_prompt)
    def _():
        emit(xp_ref)

    @pl.when(pl.program_id(0) >= n_prompt)
    def _():
        emit(xs_ref)


def _norm_call(xp, xs, g, tm=512):
    n_prompt = T_P // tm
    blk = (tm, D_MODEL)
    return pl.pallas_call(
        functools.partial(_norm_kernel, n_prompt),
        out_shape=(jax.ShapeDtypeStruct((T, D_MODEL), F32),
                   jax.ShapeDtypeStruct((T, D_MODEL), BF16)),
        grid=(T // tm,),
        in_specs=[pl.BlockSpec(blk, lambda i: (jnp.minimum(i, n_prompt - 1), 0)),
                  pl.BlockSpec(blk, lambda i: (jnp.maximum(i - n_prompt, 0), 0)),
                  pl.BlockSpec((1, D_MODEL), lambda i: (0, 0))],
        out_specs=(pl.BlockSpec(blk, lambda i: (i, 0)), pl.BlockSpec(blk, lambda i: (i, 0))),
        compiler_params=_params(("arbitrary",), [(blk, F32)] * 3 + [(blk, BF16)],
                                temps=2 * _nbytes(blk, F32)),
        name="rmsnorm_in",
    )(xp, xs, g)


def _proj_kernel(mode, *refs):
    n_w = 2 if mode == "glu" else 1
    n_extra = 2 if mode == "gelu_ln" else 0
    h_ref, w_refs = refs[0], refs[1:1 + n_w]
    extra = refs[1 + n_w:1 + n_w + n_extra]
    o_ref = refs[1 + n_w + n_extra]
    wbf = refs[2 + n_w + n_extra:2 + 2 * n_w + n_extra]
    zbuf = refs[2 + 2 * n_w + n_extra]

    i = pl.program_id(1)
    n_i = pl.num_programs(1) - 1

    def matmul():
        h = h_ref[...]
        return [_dot(h, w[...]) for w in wbf]

    def keep(zs, slot):
        for n, z in enumerate(zs):
            zbuf[slot, n] = z

    def finish(slot):
        z = zbuf[slot, 0]
        if mode == "glu":
            z = z * _sigmoid(zbuf[slot, 1])
        elif mode == "gelu":
            z = jax.nn.gelu(z)
        elif mode == "gelu_ln":
            z = _ln(jax.nn.gelu(z), extra[0][...], extra[1][...])
        elif mode == "sigmoid":
            z = _sigmoid(z)
        o_ref[...] = z.astype(o_ref.dtype)

    @pl.when(i == 0)
    def _():
        for src, dst in zip(w_refs, wbf):
            dst[...] = src[...].astype(BF16)
        keep(matmul(), 0)

    for slot in range(2):
        @pl.when(jnp.logical_and(jnp.logical_and(i > 0, i < n_i), i % 2 == slot))
        def _():
            finish(1 - slot)
            keep(matmul(), slot)

        @pl.when(jnp.logical_and(i == n_i, i % 2 == slot))
        def _():
            finish(1 - slot)


def _proj_call(h, w, layer, *, col0, ncols, mode, out_dtype, name, extra=(), tm=1088, tn=1024):
    n, k = h.shape
    hblk, wblk, oblk = (tm, k), (k, tn), (tm, tn)
    n_w = 2 if mode == "glu" else 1
    n_i = n // tm
    in_specs = [pl.BlockSpec(hblk, lambda j, i: (jnp.minimum(i, n_i - 1), 0)),
                pl.BlockSpec((None,) + wblk, lambda j, i: (layer, 0, col0 + j))]
    args = [h, w]
    if mode == "glu":
        in_specs.append(pl.BlockSpec((None,) + wblk, lambda j, i: (layer, 0, col0 + ncols + j)))
        args.append(w)
    for e in extra:
        in_specs.append(pl.BlockSpec((1, tn), lambda j, i: (0, 0)))
        args.append(e)
    return pl.pallas_call(
        functools.partial(_proj_kernel, mode),
        out_shape=jax.ShapeDtypeStruct((n, ncols * tn), out_dtype),
        grid=(ncols, n_i + 1),
        in_specs=in_specs,
        out_specs=pl.BlockSpec(oblk, lambda j, i: (jnp.maximum(i - 1, 0), j)),
        scratch_shapes=[pltpu.VMEM(wblk, BF16)] * n_w + [pltpu.VMEM((2, n_w) + oblk, F32)],
        compiler_params=_params(("arbitrary", "arbitrary"),
                                [(hblk, h.dtype), (oblk, out_dtype)] + [(wblk, F32)] * n_w,
                                resident=n_w * _nbytes(wblk, BF16) + 2 * n_w * _nbytes(oblk, F32),
                                temps=(1 + n_w) * _nbytes(oblk, F32)),
        name=name,
    )(*args)


MIX_TM = 256
MIX_RB = 32
CONV_HALO = 32
POOL_HALO = 16


def _conv_ln_silu(window, cw_ref, cb, lg, lb):
    acc = cb + cw_ref[0:1, :] * window(0)
    for k in range(1, CONV_K):
        acc = acc + cw_ref[k:k + 1, :] * window(k)
    y = _ln(acc, lg, lb)
    return y * _sigmoid(y)


def _mix_prompt_kernel(ug_ref, vn_ref, glu_ref, gh_ref, p_ref, ph_ref,
                       ws_ref, bs_ref, cw_ref, cb_ref, lg_ref, lb_ref, pw_ref, psc_ref,
                       ya_ref, yb_ref, yc_ref, gwin, gshift, pwin):
    tm = MIX_TM
    seq_tile = pl.program_id(0) % (SEQ // tm)
    first = seq_tile == 0

    row = lax.broadcasted_iota(jnp.int32, (CHUNK, CHUNK), 0)
    col = lax.broadcasted_iota(jnp.int32, (CHUNK, CHUNK), 1)
    for h in range(H_A):
        w = jnp.where(row >= col, ws_ref[h], 0.0).astype(BF16)
        hs = slice(h * DH_A, (h + 1) * DH_A)
        for c in range(tm // CHUNK):
            rs = slice(c * CHUNK, (c + 1) * CHUNK)
            mixed = _dot(w, vn_ref[rs, hs].astype(BF16)) + bs_ref[:, hs]
            ya_ref[rs, hs] = (ug_ref[rs, hs] * mixed).astype(ya_ref.dtype)

    gwin[0:CONV_HALO, :] = jnp.where(first, 0.0, gh_ref[...])
    gwin[CONV_HALO:, :] = glu_ref[...]
    n_shift = CONV_HALO + tm - SUBLANES
    for r in range(1, SUBLANES):
        gshift[r - 1, 0:n_shift, :] = gwin[r:r + n_shift, :]
    cb, lg, lb = cb_ref[...], lg_ref[...], lb_ref[...]
    off = CONV_HALO - (CONV_K - 1)

    def window(r0, k):
        a, r = divmod(off + k, SUBLANES)
        lo = r0 + SUBLANES * a
        if r == 0:
            return gwin[lo:lo + MIX_RB, :]
        return gshift[r - 1, lo:lo + MIX_RB, :]

    for rb in range(tm // MIX_RB):
        r0 = rb * MIX_RB
        y = _conv_ln_silu(functools.partial(window, r0), cw_ref, cb, lg, lb)
        yb_ref[r0:r0 + MIX_RB, :] = y.astype(yb_ref.dtype)

    pwin[0:POOL_HALO, :] = jnp.where(first, 0.0, ph_ref[...])
    pwin[POOL_HALO:, :] = p_ref[...]
    pos = seq_tile * tm + lax.broadcasted_iota(jnp.int32, (tm, 1), 0)
    for g, wlen in enumerate(POOL_WINDOWS):
        gs = slice(g * G_POOL, (g + 1) * G_POOL)
        s = pwin[POOL_HALO:POOL_HALO + tm, gs]
        for d in range(1, wlen):
            s = s + pwin[POOL_HALO - d:POOL_HALO - d + tm, gs]
        inv = 1.0 / jnp.minimum(wlen, pos + 1).astype(F32)
        mixed = s * inv - p_ref[:, gs]
        y = _dot(mixed.astype(BF16), pw_ref[g]) * psc_ref[:, gs]
        yc_ref[:, gs] = y.astype(yc_ref.dtype)


def _mix_prompt_call(ug, vn, glu, p, ws, bs_full, cw, cb, lg, lb, pw, psc):
    tm = MIX_TM
    blk = (tm, D_BR)
    row_map = lambda i: (i, 0)
    const2 = lambda i: (0, 0)
    const3 = lambda i: (0, 0, 0)
    in_specs = [
        pl.BlockSpec(blk, row_map),
        pl.BlockSpec(blk, row_map),
        pl.BlockSpec(blk, row_map),
        pl.BlockSpec((CONV_HALO, D_BR),
                     lambda i: (jnp.maximum(i * (tm // CONV_HALO) - 1, 0), 0)),
        pl.BlockSpec(blk, row_map),
        pl.BlockSpec((POOL_HALO, D_BR),
                     lambda i: (jnp.maximum(i * (tm // POOL_HALO) - 1, 0), 0)),
        pl.BlockSpec((H_A, CHUNK, CHUNK), const3),
        pl.BlockSpec((CHUNK, D_A), const2),
        pl.BlockSpec((CONV_K, D_CONV), const2),
        pl.BlockSpec((1, D_CONV), const2),
        pl.BlockSpec((1, D_CONV), const2),
        pl.BlockSpec((1, D_CONV), const2),
        pl.BlockSpec((N_POOL, G_POOL, G_POOL), const3),
        pl.BlockSpec((1, D_POOL), const2),
    ]
    out = jax.ShapeDtypeStruct((T, D_BR), BF16)
    return pl.pallas_call(
        _mix_prompt_kernel,
        out_shape=(out, out, out),
        grid=(T_P // tm,),
        in_specs=in_specs,
        out_specs=(pl.BlockSpec(blk, row_map),) * 3,
        scratch_shapes=[pltpu.VMEM((CONV_HALO + tm, D_CONV), F32),
                        pltpu.VMEM((SUBLANES - 1, CONV_HALO + tm, D_CONV), F32),
                        pltpu.VMEM((POOL_HALO + tm, D_POOL), F32)],
        compiler_params=_params(
            ("parallel",),
            [(blk, F32)] * 4 + [(blk, BF16)] * 3 + [((CONV_HALO + POOL_HALO, D_BR), F32),
                                                    ((CHUNK + CONV_K + 8, D_BR), F32)],
            resident=_nbytes((SUBLANES * (CONV_HALO + tm) + POOL_HALO + tm, D_BR), F32),
            temps=20 * _nbytes(blk, F32)),
        name="mix_prompt",
    )(ug, vn, glu, glu, p, p, ws, bs_full, cw, cb, lg, lb, pw, psc)


def _mix_sample_kernel(ug_ref, vn_ref, glu_ref, p_ref, cpast_ref, ppast_ref,
                       wsm_ref, bsm_ref, cw_ref, cb_ref, lg_ref, lb_ref, pw_ref, psc_ref,
                       ya_in, yb_in, yc_in, ya_ref, yb_ref, yc_ref, mix_scr):
    del ya_in, yb_in, yc_in
    cb, lg, lb = cb_ref[...], lg_ref[...], lb_ref[...]

    def rows(t):
        return slice(t * SB, (t + 1) * SB)

    def conv_src(j):
        return cpast_ref[j] if j < CONV_K - 1 else glu_ref[rows(j - (CONV_K - 1)), :]

    def pool_src(j, gs):
        return ppast_ref[j, :, gs] if j < POOL_PAST else p_ref[rows(j - POOL_PAST), gs]

    for t in range(DEC_SEQ):
        m = bsm_ref[t:t + 1, :] + wsm_ref[t, 0:1, :] * vn_ref[rows(0), :]
        for s in range(1, t + 1):
            m = m + wsm_ref[t, s:s + 1, :] * vn_ref[rows(s), :]
        ya_ref[rows(t), :] = (ug_ref[rows(t), :] * m).astype(ya_ref.dtype)
        y = _conv_ln_silu(lambda k: conv_src(t + k), cw_ref, cb, lg, lb)
        yb_ref[rows(t), :] = y.astype(yb_ref.dtype)
        for g, wlen in enumerate(POOL_WINDOWS):
            gs = slice(g * G_POOL, (g + 1) * G_POOL)
            s = pool_src(POOL_PAST + t, gs)
            for d in range(1, wlen):
                s = s + pool_src(POOL_PAST + t - d, gs)
            cnt = float(min(wlen, PAST_LEN + t + 1))
            mix_scr[rows(t), gs] = s / cnt - p_ref[rows(t), gs]
    for g in range(N_POOL):
        gs = slice(g * G_POOL, (g + 1) * G_POOL)
        y = _dot(mix_scr[:, gs].astype(BF16), pw_ref[g]) * psc_ref[:, gs]
        yc_ref[:, gs] = y.astype(yc_ref.dtype)


def _mix_sample_call(ug, vn, glu, p, cpast, ppast, wsm, bsm, cw, cb, lg, lb, pw, psc, ya, yb, yc):
    blk = (SBLK, D_BR)
    blk0 = T_P // SBLK
    row_map = lambda j: (blk0 + j, 0)
    const2 = lambda j: (0, 0)
    const3 = lambda j: (0, 0, 0)
    any_spec = pl.BlockSpec(memory_space=pl.ANY)
    in_specs = [
        pl.BlockSpec(blk, row_map), pl.BlockSpec(blk, row_map),
        pl.BlockSpec(blk, row_map), pl.BlockSpec(blk, row_map),
        pl.BlockSpec((CONV_K - 1, SB, D_CONV), lambda j: (0, j, 0)),
        pl.BlockSpec((POOL_PAST, SB, D_POOL), lambda j: (0, j, 0)),
        pl.BlockSpec((DEC_SEQ, DEC_SEQ, D_A), const3),
        pl.BlockSpec((DEC_SEQ, D_A), const2),
        pl.BlockSpec((CONV_K, D_CONV), const2),
        pl.BlockSpec((1, D_CONV), const2),
        pl.BlockSpec((1, D_CONV), const2),
        pl.BlockSpec((1, D_CONV), const2),
        pl.BlockSpec((N_POOL, G_POOL, G_POOL), const3),
        pl.BlockSpec((1, D_POOL), const2),
        any_spec, any_spec, any_spec,
    ]
    out = jax.ShapeDtypeStruct((T, D_BR), BF16)
    return pl.pallas_call(
        _mix_sample_kernel,
        out_shape=(out, out, out),
        grid=(N_SBLK,),
        in_specs=in_specs,
        out_specs=(pl.BlockSpec(blk, row_map),) * 3,
        scratch_shapes=[pltpu.VMEM(blk, F32)],
        input_output_aliases={14: 0, 15: 1, 16: 2},
        compiler_params=_params(
            ("parallel",),
            [(blk, F32)] * 4 + [(blk, BF16)] * 3
            + [((CONV_K - 1 + POOL_PAST, SB, D_BR), F32), ((CONV_K + 16, D_BR), F32)],
            resident=_nbytes(blk, F32), temps=32 * _nbytes(blk, F32)),
        name="mix_sample",
    )(ug, vn, glu, p, cpast, ppast, wsm, bsm, cw, cb, lg, lb, pw, psc, ya, yb, yc)


def _merge_kernel(ya_ref, yb_ref, yc_ref, g0_ref, g1_ref, g2_ref, x_ref,
                  wb_ref, wo_ref, gn_ref, xo_ref, ho_ref):
    merged = g0_ref[...] * _dot(ya_ref[...], wb_ref[0])
    merged = merged + g1_ref[...] * _dot(yb_ref[...], wb_ref[1])
    merged = merged + g2_ref[...] * _dot(yc_ref[...], wb_ref[2])
    xn = x_ref[...] + _dot(merged.astype(BF16), wo_ref[...])
    xo_ref[...] = xn
    ho_ref[...] = _rms(xn, gn_ref[...]).astype(ho_ref.dtype)


def _merge_call(ya, yb, yc, gates, x, wb, wo, layer, gn, tm=256):
    yblk, xblk = (tm, D_BR), (tm, D_MODEL)
    row_map = lambda i: (i, 0)
    in_specs = [
        pl.BlockSpec(yblk, row_map), pl.BlockSpec(yblk, row_map), pl.BlockSpec(yblk, row_map),
        pl.BlockSpec(xblk, lambda i: (i, 0)),
        pl.BlockSpec(xblk, lambda i: (i, 1)),
        pl.BlockSpec(xblk, lambda i: (i, 2)),
        pl.BlockSpec(xblk, row_map),
        _resident((None, N_BRANCH, D_BR, D_MODEL), lambda i: (layer, 0, 0, 0)),
        _resident((None, D_MODEL, D_MODEL), lambda i: (layer, 0, 0)),
        pl.BlockSpec((1, D_MODEL), lambda i: (0, 0)),
    ]
    return pl.pallas_call(
        _merge_kernel,
        out_shape=(jax.ShapeDtypeStruct((T, D_MODEL), F32),
                   jax.ShapeDtypeStruct((T, D_MODEL), BF16)),
        grid=(T // tm,),
        in_specs=in_specs,
        out_specs=(pl.BlockSpec(xblk, row_map), pl.BlockSpec(xblk, row_map)),
        compiler_params=_params(
            ("parallel",),
            [(yblk, BF16)] * 3 + [(xblk, gates.dtype)] * 3 + [(xblk, F32)] * 2 + [(xblk, BF16)],
            resident=_nbytes((N_BRANCH, D_BR, D_MODEL), BF16) + _nbytes((D_MODEL, D_MODEL), BF16),
            temps=3 * _nbytes(xblk, F32)),
        name="merge_out",
    )(ya, yb, yc, gates, gates, gates, x, wb, wo, gn)


def _memkv_kernel(m_ref, g_ref, wk_ref, wv_ref, k32_ref, v32_ref, k16_ref, v16_ref):
    m = _rms(m_ref[...], g_ref[...]).astype(BF16)
    k = _dot(m, wk_ref[...])
    v = _dot(m, wv_ref[...])
    k32_ref[...] = k
    v32_ref[...] = v
    k16_ref[...] = k.astype(BF16)
    v16_ref[...] = v.astype(BF16)


def _memkv_call(mem, g, wk, wv, layer, tn=1024):
    n = mem.shape[0]
    mblk, wblk, oblk = (n, D_MODEL), (D_MODEL, tn), (n, tn)
    o32 = jax.ShapeDtypeStruct((n, D_MODEL), F32)
    o16 = jax.ShapeDtypeStruct((n, D_MODEL), BF16)
    col = lambda j: (0, j)
    wcol = lambda j: (layer, 0, j)
    return pl.pallas_call(
        _memkv_kernel,
        out_shape=(o32, o32, o16, o16),
        grid=(D_MODEL // tn,),
        in_specs=[_resident(mblk, lambda j: (0, 0)),
                  pl.BlockSpec((1, D_MODEL), lambda j: (0, 0)),
                  pl.BlockSpec((None,) + wblk, wcol), pl.BlockSpec((None,) + wblk, wcol)],
        out_specs=(pl.BlockSpec(oblk, col),) * 4,
        compiler_params=_params(
            ("parallel",),
            [(wblk, BF16)] * 2 + [(oblk, F32)] * 2 + [(oblk, BF16)] * 2,
            resident=_nbytes(mblk, F32), temps=2 * _nbytes(mblk, F32)),
        name="mem_kv",
    )(mem, g, wk, wv)


_NT = (((1,), (1,)), ((), ()))


def _softmax_rows(s):
    e = jnp.exp(s - jnp.max(s, axis=-1, keepdims=True))
    return e * (1.0 / jnp.sum(e, axis=-1, keepdims=True))


def _xattn_prompt_kernel(q_ref, k_ref, v_ref, o_ref):
    scale = X_HEAD_DIM ** -0.5
    for h in range(X_HEADS):
        hs = slice(h * X_HEAD_DIM, (h + 1) * X_HEAD_DIM)
        s = lax.dot_general(q_ref[:, hs], k_ref[:, hs], _NT, preferred_element_type=F32) * scale
        pr = _softmax_rows(s).astype(BF16)
        o_ref[:, hs] = _dot(pr, v_ref[:, hs]).astype(o_ref.dtype)


def _xattn_prompt_call(q, k16, v16, tq=512):
    qblk, kblk = (tq, D_MODEL), (MEM_LEN, D_MODEL)
    per_seq = SEQ // tq
    return pl.pallas_call(
        _xattn_prompt_kernel,
        out_shape=jax.ShapeDtypeStruct((T, D_MODEL), BF16),
        grid=(BATCH, per_seq),
        in_specs=[pl.BlockSpec(qblk, lambda b, s: (b * per_seq + s, 0)),
                  pl.BlockSpec(kblk, lambda b, s: (b, 0)),
                  pl.BlockSpec(kblk, lambda b, s: (b, 0))],
        out_specs=pl.BlockSpec(qblk, lambda b, s: (b * per_seq + s, 0)),
        compiler_params=_params(("parallel", "parallel"),
                                [(qblk, BF16)] * 2 + [(kblk, BF16)] * 2,
                                temps=4 * _nbytes((tq, MEM_LEN), F32) + _nbytes(qblk, F32)),
        name="xattn_prompt",
    )(q, k16, v16)


LANES = 128
HEAD_LANE_TILES = X_HEAD_DIM // LANES
KV_ROWS = MEM_LEN * HEAD_LANE_TILES * X_HEADS
KV_BB = 4


def _cache_rows(c):
    d, b = c.shape[:2]
    c = c.reshape(d, b, MEM_LEN, X_HEADS, HEAD_LANE_TILES, LANES).transpose(0, 1, 2, 4, 3, 5)
    return c.reshape(d, b, KV_ROWS, LANES)


def _head_rows(ref, i, h):
    stride = HEAD_LANE_TILES * X_HEADS
    tiles = [ref[i, pl.ds(t * X_HEADS + h, MEM_LEN, stride=stride), :]
             for t in range(HEAD_LANE_TILES)]
    return jnp.concatenate(tiles, axis=1)


def _xattn_sample_kernel(q_ref, k_ref, v_ref, o_in, o_ref, acc):
    del o_in
    c = pl.program_id(1)
    scale = X_HEAD_DIM ** -0.5

    @pl.when(c == 0)
    def _():
        acc[...] = jnp.zeros_like(acc)

    owner = lax.broadcasted_iota(jnp.int32, (SBLK, MEM_LEN), 0) % SB
    for h in range(X_HEADS):
        hs = slice(h * X_HEAD_DIM, (h + 1) * X_HEAD_DIM)
        keys = jnp.concatenate([_head_rows(k_ref, ii, h) for ii in range(KV_BB)], axis=0)
        s = lax.dot_general(q_ref[:, hs], keys.astype(BF16), _NT,
                            preferred_element_type=F32) * scale
        pr = [jnp.where(owner == c * KV_BB + ii,
                        _softmax_rows(s[:, ii * MEM_LEN:(ii + 1) * MEM_LEN]), 0.0)
              for ii in range(KV_BB)]
        vals = jnp.concatenate([_head_rows(v_ref, ii, h) for ii in range(KV_BB)], axis=0)
        acc[:, hs] += _dot(jnp.concatenate(pr, axis=1).astype(BF16), vals.astype(BF16))

    @pl.when(c == pl.num_programs(1) - 1)
    def _():
        o_ref[...] = acc[...].astype(o_ref.dtype)


def _xattn_sample_call(q, mk, mv, layer, o):
    qblk = (SBLK, D_MODEL)
    kblk = (None, KV_BB, KV_ROWS, LANES)
    blk0 = T_P // SBLK
    per_blk = SB // KV_BB
    return pl.pallas_call(
        _xattn_sample_kernel,
        out_shape=jax.ShapeDtypeStruct((T, D_MODEL), BF16),
        grid=(N_SBLK, per_blk),
        in_specs=[pl.BlockSpec(qblk, lambda j, c: (blk0 + j, 0)),
                  pl.BlockSpec(kblk, lambda j, c: (layer, j * per_blk + c, 0, 0)),
                  pl.BlockSpec(kblk, lambda j, c: (layer, j * per_blk + c, 0, 0)),
                  pl.BlockSpec(memory_space=pl.ANY)],
        out_specs=pl.BlockSpec(qblk, lambda j, c: (blk0 + j, 0)),
        scratch_shapes=[pltpu.VMEM(qblk, F32)],
        input_output_aliases={3: 0},
        compiler_params=_params(("parallel", "arbitrary"),
                                [(qblk, BF16)] * 2 + [((KV_BB, KV_ROWS, LANES), F32)] * 2,
                                resident=_nbytes(qblk, F32),
                                temps=4 * KV_BB * _nbytes((MEM_LEN, X_HEAD_DIM), F32)),
        name="xattn_sample",
    )(q, mk, mv, o)


def _resid_kernel(a_ref, w_ref, x_ref, gn_ref, xo_ref, ho_ref):
    xn = x_ref[...] + _dot(a_ref[...], w_ref[...])
    xo_ref[...] = xn
    ho_ref[...] = _rms(xn, gn_ref[...]).astype(ho_ref.dtype)


def _resid_call(a, w, layer, x, gn, tm=512):
    blk = (tm, D_MODEL)
    row_map = lambda i: (i, 0)
    return pl.pallas_call(
        _resid_kernel,
        out_shape=(jax.ShapeDtypeStruct((T, D_MODEL), F32),
                   jax.ShapeDtypeStruct((T, D_MODEL), BF16)),
        grid=(T // tm,),
        in_specs=[pl.BlockSpec(blk, row_map),
                  _resident((None, D_MODEL, D_MODEL), lambda i: (layer, 0, 0)),
                  pl.BlockSpec(blk, row_map),
                  pl.BlockSpec((1, D_MODEL), lambda i: (0, 0))],
        out_specs=(pl.BlockSpec(blk, row_map), pl.BlockSpec(blk, row_map)),
        compiler_params=_params(("parallel",),
                                [(blk, BF16)] * 2 + [(blk, F32)] * 2,
                                resident=_nbytes((D_MODEL, D_MODEL), BF16),
                                temps=2 * _nbytes(blk, F32)),
        name="attn_out",
    )(a, w, x, gn)


def _mlp_kernel(n_prompt, n_f, h_ref, wu_ref, wd_ref, x_ref, gn_ref, o0_ref, o1_ref, acc, hid):
    i, f = pl.program_id(0), pl.program_id(1)

    @pl.when(f == 0)
    def _():
        acc[...] = x_ref[...]


    def up():
        u = jnp.maximum(_dot(h_ref[...], wu_ref[...]), 0.0)
        return (u * u).astype(BF16)

    def down(slot):
        acc[...] += _dot(hid[slot], wd_ref[...])

    @pl.when(f == 0)
    def _():
        hid[0] = up()

    for slot in range(2):
        @pl.when(jnp.logical_and(jnp.logical_and(f > 0, f < n_f), f % 2 == slot))
        def _():
            down(1 - slot)
            hid[slot] = up()

    @pl.when(f == n_f)
    def _():
        down((n_f - 1) % 2)
        xn = acc[...]
        if n_prompt is None:
            o0_ref[...] = xn
            o1_ref[...] = _rms(xn, gn_ref[...]).astype(o1_ref.dtype)
        else:
            y = _rms(xn, gn_ref[...])

            @pl.when(i < n_prompt)
            def _():
                o0_ref[...] = y

            @pl.when(i >= n_prompt)
            def _():
                o1_ref[...] = y


def _mlp_call(h, wu, wd, layer, x, gn, final, tm=512, tf=1024):
    blk = (tm, D_MODEL)
    row_map = lambda i, f: (i, 0)
    n_f = D_FF // tf
    if final:
        n_prompt = T_P // tm
        out_shape = (jax.ShapeDtypeStruct((T_P, D_MODEL), F32),
                     jax.ShapeDtypeStruct((T_S, D_MODEL), F32))
        out_specs = (pl.BlockSpec(blk, lambda i, f: (jnp.minimum(i, n_prompt - 1), 0)),
                     pl.BlockSpec(blk, lambda i, f: (jnp.maximum(i - n_prompt, 0), 0)))
        out_dtype = F32
    else:
        n_prompt = None
        out_shape = (jax.ShapeDtypeStruct((T, D_MODEL), F32),
                     jax.ShapeDtypeStruct((T, D_MODEL), BF16))
        out_specs = (pl.BlockSpec(blk, row_map), pl.BlockSpec(blk, row_map))
        out_dtype = BF16
    return pl.pallas_call(
        functools.partial(_mlp_kernel, n_prompt, n_f),
        out_shape=out_shape,
        grid=(T // tm, n_f + 1),
        in_specs=[pl.BlockSpec(blk, row_map),
                  pl.BlockSpec((None, D_MODEL, tf),
                               lambda i, f: (layer, 0, jnp.minimum(f, n_f - 1))),
                  pl.BlockSpec((None, tf, D_MODEL),
                               lambda i, f: (layer, jnp.maximum(f - 1, 0), 0)),
                  pl.BlockSpec(blk, row_map),
                  pl.BlockSpec((1, D_MODEL), lambda i, f: (0, 0))],
        out_specs=out_specs,
        scratch_shapes=[pltpu.VMEM(blk, F32), pltpu.VMEM((2, tm, tf), BF16)],
        compiler_params=_params(
            ("arbitrary", "arbitrary"),
            [(blk, BF16), ((D_MODEL, tf), BF16), ((tf, D_MODEL), BF16), (blk, F32), (blk, F32),
             (blk, out_dtype)],
            resident=_nbytes(blk, F32) + _nbytes((2, tm, tf), BF16),
            temps=2 * _nbytes((tm, tf), F32) + _nbytes(blk, F32)),
        name="mlp_final" if final else "mlp",
    )(h, wu, wd, x, gn)


def _sample_to_rows(a):
    c = a.shape[-1]
    return a.reshape(N_SBLK, SB, DEC_SEQ, c).transpose(0, 2, 1, 3).reshape(T_S, c)


def _rows_to_sample(a):
    c = a.shape[-1]
    return a.reshape(N_SBLK, DEC_SEQ, SB, c).transpose(0, 2, 1, 3).reshape(DEC_BATCH, DEC_SEQ, c)


def kernel(x_prompt, x_sample, mem_prompt, cache_mem_k, cache_mem_v, state_conv, state_pool,
           g_mix, w_in, ln_v_g, ln_v_b, w_s, b_s, conv_w, conv_b, ln_c_g, ln_c_b,
           pool_w, pool_scale, w_branch, w_out, g_xattn, g_mem, w_xq, w_xk, w_xv, w_xo,
           g_mlp, w_up, w_down, g_final):
    assert T_S == 512 and T_P % 512 == 0
    row = lambda a: a.reshape(1, -1)
    mem = mem_prompt.reshape(BATCH * MEM_LEN, D_MODEL)
    cache_k, cache_v = _cache_rows(cache_mem_k), _cache_rows(cache_mem_v)
    wb16, wo16 = w_branch.astype(BF16), w_out.astype(BF16)
    wk16, wv16, wxo16 = w_xk.astype(BF16), w_xv.astype(BF16), w_xo.astype(BF16)
    wu16, wd16, pw16 = w_up.astype(BF16), w_down.astype(BF16), pool_w.astype(BF16)
    n1 = D_BR // 1024

    x, h = _norm_call(x_prompt.reshape(T_P, D_MODEL), _sample_to_rows(x_sample), row(g_mix[0]))
    mk_p, mv_p, conv_p, pool_p, conv_s, pool_s, chunk_s = [], [], [], [], [], [], []
    y_prompt = y_sample = None
    for l in range(DEPTH):
        ug = _proj_call(h, w_in, l, col0=0, ncols=n1, mode="gelu", out_dtype=F32, name="proj_u")
        vn = _proj_call(h, w_in, l, col0=n1, ncols=n1, mode="gelu_ln", out_dtype=F32,
                        name="proj_v", extra=(row(ln_v_g[l]), row(ln_v_b[l])))
        glu = _proj_call(h, w_in, l, col0=2 * D_A // 512, ncols=D_CONV // 512, mode="glu",
                         out_dtype=F32, name="proj_glu", tn=512)
        p = _proj_call(h, w_in, l, col0=4 * n1, ncols=n1, mode="id", out_dtype=F32,
                       name="proj_p")
        gates = _proj_call(h, w_in, l, col0=5 * n1, ncols=N_BRANCH * D_MODEL // 1024,
                           mode="sigmoid", out_dtype=F32, name="proj_gate")

        cw, cb = conv_w[l], row(conv_b[l])
        lg, lb = row(ln_c_g[l]), row(ln_c_b[l])
        pw, psc = pw16[l], row(pool_scale[l])
        bs_full = jnp.repeat(b_s[l].T, DH_A, axis=1)
        ya, yb, yc = _mix_prompt_call(ug, vn, glu, p, w_s[l], bs_full, cw, cb, lg, lb, pw, psc)
        wsm = jnp.repeat(w_s[l][:, :DEC_SEQ, :DEC_SEQ].transpose(1, 2, 0), DH_A, axis=2)
        bsm = jnp.repeat(b_s[l][:, :DEC_SEQ].T, DH_A, axis=1)
        cpast = state_conv[l].transpose(1, 0, 2)
        ppast = state_pool[l].transpose(1, 0, 2)
        ya, yb, yc = _mix_sample_call(ug, vn, glu, p, cpast, ppast, wsm, bsm, cw, cb, lg, lb,
                                      pw, psc, ya, yb, yc)

        x, hx = _merge_call(ya, yb, yc, gates, x, wb16, wo16, l, row(g_xattn[l]))

        k32, v32, k16, v16 = _memkv_call(mem, row(g_mem[l]), wk16, wv16, l)
        q = _proj_call(hx, w_xq, l, col0=0, ncols=D_MODEL // 1024, mode="id", out_dtype=BF16,
                       name="proj_q")
        o = _xattn_prompt_call(q, k16, v16)
        o = _xattn_sample_call(q, cache_k, cache_v, l, o)
        x, hm = _resid_call(o, wxo16, l, x, row(g_mlp[l]))

        if l == DEPTH - 1:
            y_prompt, y_sample = _mlp_call(hm, wu16, wd16, l, x, row(g_final), True)
        else:
            x, h = _mlp_call(hm, wu16, wd16, l, x, row(g_mix[l + 1]), False)

        mk_p.append(k32.reshape(BATCH, MEM_LEN, X_HEADS, X_HEAD_DIM))
        mv_p.append(v32.reshape(BATCH, MEM_LEN, X_HEADS, X_HEAD_DIM))
        conv_p.append(glu[:T_P].reshape(BATCH, SEQ, D_CONV)[:, SEQ - (CONV_K - 1):])
        pool_p.append(p[:T_P].reshape(BATCH, SEQ, D_POOL)[:, SEQ - POOL_PAST:])
        conv_s.append(jnp.concatenate([state_conv[l][:, DEC_SEQ:], _rows_to_sample(glu[T_P:])],
                                      axis=1))
        pool_s.append(jnp.concatenate([state_pool[l][:, DEC_SEQ:], _rows_to_sample(p[T_P:])],
                                      axis=1))
        chunk_s.append(_rows_to_sample(vn[T_P:]))

    y_prompt = y_prompt.reshape(BATCH, SEQ, D_MODEL)
    y_sample = _rows_to_sample(y_sample)
    return (y_prompt, y_sample, jnp.stack(mk_p), jnp.stack(mv_p), jnp.stack(conv_p),
            jnp.stack(pool_p), jnp.stack(conv_s), jnp.stack(pool_s), jnp.stack(chunk_s))
```

```python
import functools

import jax
import jax.numpy as jnp
from jax import lax
from jax.experimental import pallas as pl
from jax.experimental.pallas import tpu as pltpu

D_MODEL = 2048
BATCH = 4
SEQ = 2048
DEPTH = 2
DEC_BATCH = 128
DEC_SEQ = 4
PAST_LEN = 16384
D_BR = D_MODEL // 2
D_A = D_BR
H_A = 4
DH_A = D_A // H_A
CHUNK = 128
D_CONV = D_BR
CONV_K = 31
D_POOL = D_BR
POOL_WINDOWS = (2, 4, 8, 16)
N_POOL = len(POOL_WINDOWS)
G_POOL = D_POOL // N_POOL
POOL_PAST = max(POOL_WINDOWS) - 1
MEM_LEN = 256
X_HEADS = 4
X_HEAD_DIM = D_MODEL // X_HEADS
D_FF = 4 * D_MODEL
N_BRANCH = 3
D_IN = 2 * D_A + 2 * D_CONV + D_POOL + N_BRANCH * D_MODEL
RMS_EPS = 1e-6
LN_EPS = 1e-5

T_P = BATCH * SEQ
T_S = DEC_BATCH * DEC_SEQ
T = T_P + T_S
SB = 16
SBLK = DEC_SEQ * SB
N_SBLK = DEC_BATCH // SB

SUBLANES = 8
VMEM_PHYSICAL_BYTES = 64 * 1024 * 1024
VMEM_CEILING_BYTES = VMEM_PHYSICAL_BYTES - 6 * 1024 * 1024

F32 = jnp.float32
BF16 = jnp.bfloat16


def _nbytes(shape, dtype):
    n = 1
    for s in shape:
        n *= s
    return n * jnp.dtype(dtype).itemsize


def _params(semantics, pipelined, resident=0, temps=0):
    need = 2 * sum(_nbytes(s, d) for s, d in pipelined) + resident + temps
    need = need + need // 8 + (2 << 20)
    return pltpu.CompilerParams(
        dimension_semantics=semantics,
        vmem_limit_bytes=int(min(max(need, 16 << 20), VMEM_CEILING_BYTES)))


def _resident(shape, index_map):
    return pl.BlockSpec(shape, index_map, pipeline_mode=pl.Buffered(1))


def _sigmoid(x):
    return 0.5 * jnp.tanh(0.5 * x) + 0.5


def _rms(x, g):
    return x * lax.rsqrt(jnp.mean(x * x, axis=-1, keepdims=True) + RMS_EPS) * g


def _ln(x, g, b):
    mu = jnp.mean(x, axis=-1, keepdims=True)
    xc = x - mu
    var = jnp.mean(xc * xc, axis=-1, keepdims=True)
    return xc * lax.rsqrt(var + LN_EPS) * g + b


def _dot(a, b):
    return jnp.dot(a, b, preferred_element_type=F32)


def _norm_kernel(n_prompt, xp_ref, xs_ref, g_ref, x_ref, h_ref):
    def emit(src):
        x = src[...]
        x_ref[...] = x
        h_ref[...] = _rms(x, g_ref[...]).astype(h_ref.dtype)

    @pl.when(pl.program_id(0) < n_prompt)
    def _():
        emit(xp_ref)

    @pl.when(pl.program_id(0) >= n_prompt)
    def _():
        emit(xs_ref)


def _norm_call(xp, xs, g, tm=512):
    n_prompt = T_P // tm
    blk = (tm, D_MODEL)
    return pl.pallas_call(
        functools.partial(_norm_kernel, n_prompt),
        out_shape=(jax.ShapeDtypeStruct((T, D_MODEL), F32),
                   jax.ShapeDtypeStruct((T, D_MODEL), BF16)),
        grid=(T // tm,),
        in_specs=[pl.BlockSpec(blk, lambda i: (jnp.minimum(i, n_prompt - 1), 0)),
                  pl.BlockSpec(blk, lambda i: (jnp.maximum(i - n_prompt, 0), 0)),
                  pl.BlockSpec((1, D_MODEL), lambda i: (0, 0))],
        out_specs=(pl.BlockSpec(blk, lambda i: (i, 0)), pl.BlockSpec(blk, lambda i: (i, 0))),
        compiler_params=_params(("arbitrary",), [(blk, F32)] * 3 + [(blk, BF16)],
                                temps=2 * _nbytes(blk, F32)),
        name="rmsnorm_in",
    )(xp, xs, g)


def _proj_kernel(mode, *refs):
    n_w = 2 if mode == "glu" else 1
    n_extra = 2 if mode == "gelu_ln" else 0
    h_ref, w_refs = refs[0], refs[1:1 + n_w]
    extra = refs[1 + n_w:1 + n_w + n_extra]
    o_ref = refs[1 + n_w + n_extra]
    wbf = refs[2 + n_w + n_extra:]

    @pl.when(pl.program_id(1) == 0)
    def _():
        for src, dst in zip(w_refs, wbf):
            dst[...] = src[...].astype(BF16)

    h = h_ref[...]
    z = _dot(h, wbf[0][...])
    if mode == "glu":
        z = z * _sigmoid(_dot(h, wbf[1][...]))
    elif mode == "gelu":
        z = jax.nn.gelu(z)
    elif mode == "gelu_ln":
        z = _ln(jax.nn.gelu(z), extra[0][...], extra[1][...])
    elif mode == "sigmoid":
        z = _sigmoid(z)
    o_ref[...] = z.astype(o_ref.dtype)


def _proj_call(h, w, layer, *, col0, ncols, mode, out_dtype, name, extra=(), tm=1088, tn=1024):
    n, k = h.shape
    hblk, wblk, oblk = (tm, k), (k, tn), (tm, tn)
    n_w = 2 if mode == "glu" else 1
    in_specs = [pl.BlockSpec(hblk, lambda j, i: (i, 0)),
                pl.BlockSpec((None,) + wblk, lambda j, i: (layer, 0, col0 + j))]
    args = [h, w]
    if mode == "glu":
        in_specs.append(pl.BlockSpec((None,) + wblk, lambda j, i: (layer, 0, col0 + ncols + j)))
        args.append(w)
    for e in extra:
        in_specs.append(pl.BlockSpec((1, tn), lambda j, i: (0, 0)))
        args.append(e)
    return pl.pallas_call(
        functools.partial(_proj_kernel, mode),
        out_shape=jax.ShapeDtypeStruct((n, ncols * tn), out_dtype),
        grid=(ncols, n // tm),
        in_specs=in_specs,
        out_specs=pl.BlockSpec(oblk, lambda j, i: (i, j)),
        scratch_shapes=[pltpu.VMEM(wblk, BF16)] * n_w,
        compiler_params=_params(("arbitrary", "arbitrary"),
                                [(hblk, h.dtype), (oblk, out_dtype)] + [(wblk, F32)] * n_w,
                                resident=n_w * _nbytes(wblk, BF16),
                                temps=(2 + n_w) * _nbytes(oblk, F32)),
        name=name,
    )(*args)


MIX_TM = 256
MIX_RB = 32
CONV_HALO = 32
POOL_HALO = 16


def _conv_ln_silu(window, cw_ref, cb, lg, lb):
    acc = cb + cw_ref[0:1, :] * window(0)
    for k in range(1, CONV_K):
        acc = acc + cw_ref[k:k + 1, :] * window(k)
    y = _ln(acc, lg, lb)
    return y * _sigmoid(y)


def _mix_prompt_kernel(*refs):
    outs = refs[14:17]
    is_prompt = pl.program_id(0) < T_P // MIX_TM

    @pl.when(is_prompt)
    def _():
        _mix_prompt_tile(*refs)

    @pl.when(jnp.logical_not(is_prompt))
    def _():
        for o_ref in outs:
            o_ref[...] = jnp.zeros_like(o_ref)


def _mix_prompt_tile(ug_ref, vn_ref, glu_ref, gh_ref, p_ref, ph_ref,
                     ws_ref, bs_ref, cw_ref, cb_ref, lg_ref, lb_ref, pw_ref, psc_ref,
                     ya_ref, yb_ref, yc_ref, gwin, gshift, pwin):
    tm = MIX_TM
    seq_tile = pl.program_id(0) % (SEQ // tm)
    first = seq_tile == 0

    row = lax.broadcasted_iota(jnp.int32, (CHUNK, CHUNK), 0)
    col = lax.broadcasted_iota(jnp.int32, (CHUNK, CHUNK), 1)
    for h in range(H_A):
        w = jnp.where(row >= col, ws_ref[h], 0.0).astype(BF16)
        hs = slice(h * DH_A, (h + 1) * DH_A)
        for c in range(tm // CHUNK):
            rs = slice(c * CHUNK, (c + 1) * CHUNK)
            mixed = _dot(w, vn_ref[rs, hs].astype(BF16)) + bs_ref[:, hs]
            ya_ref[rs, hs] = (ug_ref[rs, hs] * mixed).astype(ya_ref.dtype)

    gwin[0:CONV_HALO, :] = jnp.where(first, 0.0, gh_ref[...])
    gwin[CONV_HALO:, :] = glu_ref[...]
    n_shift = CONV_HALO + tm - SUBLANES
    for r in range(1, SUBLANES):
        gshift[r - 1, 0:n_shift, :] = gwin[r:r + n_shift, :]
    cb, lg, lb = cb_ref[...], lg_ref[...], lb_ref[...]
    off = CONV_HALO - (CONV_K - 1)

    def window(r0, k):
        a, r = divmod(off + k, SUBLANES)
        lo = r0 + SUBLANES * a
        if r == 0:
            return gwin[lo:lo + MIX_RB, :]
        return gshift[r - 1, lo:lo + MIX_RB, :]

    for rb in range(tm // MIX_RB):
        r0 = rb * MIX_RB
        y = _conv_ln_silu(functools.partial(window, r0), cw_ref, cb, lg, lb)
        yb_ref[r0:r0 + MIX_RB, :] = y.astype(yb_ref.dtype)

    pwin[0:POOL_HALO, :] = jnp.where(first, 0.0, ph_ref[...])
    pwin[POOL_HALO:, :] = p_ref[...]
    pos = seq_tile * tm + lax.broadcasted_iota(jnp.int32, (tm, 1), 0)
    for g, wlen in enumerate(POOL_WINDOWS):
        gs = slice(g * G_POOL, (g + 1) * G_POOL)
        s = pwin[POOL_HALO:POOL_HALO + tm, gs]
        for d in range(1, wlen):
            s = s + pwin[POOL_HALO - d:POOL_HALO - d + tm, gs]
        inv = 1.0 / jnp.minimum(wlen, pos + 1).astype(F32)
        mixed = s * inv - p_ref[:, gs]
        y = _dot(mixed.astype(BF16), pw_ref[g]) * psc_ref[:, gs]
        yc_ref[:, gs] = y.astype(yc_ref.dtype)


def _mix_prompt_call(ug, vn, glu, p, ws, bs_full, cw, cb, lg, lb, pw, psc):
    tm = MIX_TM
    blk = (tm, D_BR)
    last = T_P // tm - 1
    row_map = lambda i: (i, 0)
    in_map = lambda i: (jnp.minimum(i, last), 0)
    const2 = lambda i: (0, 0)
    const3 = lambda i: (0, 0, 0)
    in_specs = [
        pl.BlockSpec(blk, in_map),
        pl.BlockSpec(blk, in_map),
        pl.BlockSpec(blk, in_map),
        pl.BlockSpec((CONV_HALO, D_BR), lambda i: (
            jnp.maximum(jnp.minimum(i, last) * (tm // CONV_HALO) - 1, 0), 0)),
        pl.BlockSpec(blk, in_map),
        pl.BlockSpec((POOL_HALO, D_BR), lambda i: (
            jnp.maximum(jnp.minimum(i, last) * (tm // POOL_HALO) - 1, 0), 0)),
        pl.BlockSpec((H_A, CHUNK, CHUNK), const3),
        pl.BlockSpec((CHUNK, D_A), const2),
        pl.BlockSpec((CONV_K, D_CONV), const2),
        pl.BlockSpec((1, D_CONV), const2),
        pl.BlockSpec((1, D_CONV), const2),
        pl.BlockSpec((1, D_CONV), const2),
        pl.BlockSpec((N_POOL, G_POOL, G_POOL), const3),
        pl.BlockSpec((1, D_POOL), const2),
    ]
    out = jax.ShapeDtypeStruct((T, D_BR), BF16)
    return pl.pallas_call(
        _mix_prompt_kernel,
        out_shape=(out, out, out),
        grid=(T // tm,),
        in_specs=in_specs,
        out_specs=(pl.BlockSpec(blk, row_map),) * 3,
        scratch_shapes=[pltpu.VMEM((CONV_HALO + tm, D_CONV), F32),
                        pltpu.VMEM((SUBLANES - 1, CONV_HALO + tm, D_CONV), F32),
                        pltpu.VMEM((POOL_HALO + tm, D_POOL), F32)],
        compiler_params=_params(
            ("parallel",),
            [(blk, F32)] * 4 + [(blk, BF16)] * 3 + [((CONV_HALO + POOL_HALO, D_BR), F32),
                                                    ((CHUNK + CONV_K + 8, D_BR), F32)],
            resident=_nbytes((SUBLANES * (CONV_HALO + tm) + POOL_HALO + tm, D_BR), F32),
            temps=20 * _nbytes(blk, F32)),
        name="mix_prompt",
    )(ug, vn, glu, glu, p, p, ws, bs_full, cw, cb, lg, lb, pw, psc)


def _mix_sample_kernel(ug_ref, vn_ref, glu_ref, p_ref, cpast_ref, ppast_ref,
                       wsm_ref, bsm_ref, cw_ref, cb_ref, lg_ref, lb_ref, pw_ref, psc_ref,
                       ya_in, yb_in, yc_in, ya_ref, yb_ref, yc_ref, mix_scr):
    del ya_in, yb_in, yc_in
    cb, lg, lb = cb_ref[...], lg_ref[...], lb_ref[...]

    def rows(t):
        return slice(t * SB, (t + 1) * SB)

    def conv_src(j):
        return cpast_ref[j] if j < CONV_K - 1 else glu_ref[rows(j - (CONV_K - 1)), :]

    def pool_src(j, gs):
        return ppast_ref[j, :, gs] if j < POOL_PAST else p_ref[rows(j - POOL_PAST), gs]

    for t in range(DEC_SEQ):
        m = bsm_ref[t:t + 1, :] + wsm_ref[t, 0:1, :] * vn_ref[rows(0), :]
        for s in range(1, t + 1):
            m = m + wsm_ref[t, s:s + 1, :] * vn_ref[rows(s), :]
        ya_ref[rows(t), :] = (ug_ref[rows(t), :] * m).astype(ya_ref.dtype)
        y = _conv_ln_silu(lambda k: conv_src(t + k), cw_ref, cb, lg, lb)
        yb_ref[rows(t), :] = y.astype(yb_ref.dtype)
        for g, wlen in enumerate(POOL_WINDOWS):
            gs = slice(g * G_POOL, (g + 1) * G_POOL)
            s = pool_src(POOL_PAST + t, gs)
            for d in range(1, wlen):
                s = s + pool_src(POOL_PAST + t - d, gs)
            cnt = float(min(wlen, PAST_LEN + t + 1))
            mix_scr[rows(t), gs] = s / cnt - p_ref[rows(t), gs]
    for g in range(N_POOL):
        gs = slice(g * G_POOL, (g + 1) * G_POOL)
        y = _dot(mix_scr[:, gs].astype(BF16), pw_ref[g]) * psc_ref[:, gs]
        yc_ref[:, gs] = y.astype(yc_ref.dtype)


def _mix_sample_call(ug, vn, glu, p, cpast, ppast, wsm, bsm, cw, cb, lg, lb, pw, psc, ya, yb, yc):
    blk = (SBLK, D_BR)
    blk0 = T_P // SBLK
    row_map = lambda j: (blk0 + j, 0)
    const2 = lambda j: (0, 0)
    const3 = lambda j: (0, 0, 0)
    any_spec = pl.BlockSpec(memory_space=pl.ANY)
    in_specs = [
        pl.BlockSpec(blk, row_map), pl.BlockSpec(blk, row_map),
        pl.BlockSpec(blk, row_map), pl.BlockSpec(blk, row_map),
        pl.BlockSpec((CONV_K - 1, SB, D_CONV), lambda j: (0, j, 0)),
        pl.BlockSpec((POOL_PAST, SB, D_POOL), lambda j: (0, j, 0)),
        pl.BlockSpec((DEC_SEQ, DEC_SEQ, D_A), const3),
        pl.BlockSpec((DEC_SEQ, D_A), const2),
        pl.BlockSpec((CONV_K, D_CONV), const2),
        pl.BlockSpec((1, D_CONV), const2),
        pl.BlockSpec((1, D_CONV), const2),
        pl.BlockSpec((1, D_CONV), const2),
        pl.BlockSpec((N_POOL, G_POOL, G_POOL), const3),
        pl.BlockSpec((1, D_POOL), const2),
        any_spec, any_spec, any_spec,
    ]
    out = jax.ShapeDtypeStruct((T, D_BR), BF16)
    return pl.pallas_call(
        _mix_sample_kernel,
        out_shape=(out, out, out),
        grid=(N_SBLK,),
        in_specs=in_specs,
        out_specs=(pl.BlockSpec(blk, row_map),) * 3,
        scratch_shapes=[pltpu.VMEM(blk, F32)],
        input_output_aliases={14: 0, 15: 1, 16: 2},
        compiler_params=_params(
            ("parallel",),
            [(blk, F32)] * 4 + [(blk, BF16)] * 3
            + [((CONV_K - 1 + POOL_PAST, SB, D_BR), F32), ((CONV_K + 16, D_BR), F32)],
            resident=_nbytes(blk, F32), temps=32 * _nbytes(blk, F32)),
        name="mix_sample",
    )(ug, vn, glu, p, cpast, ppast, wsm, bsm, cw, cb, lg, lb, pw, psc, ya, yb, yc)


def _merge_kernel(ya_ref, yb_ref, yc_ref, g0_ref, g1_ref, g2_ref, x_ref,
                  wb_ref, wo_ref, gn_ref, xo_ref, ho_ref):
    merged = g0_ref[...] * _dot(ya_ref[...], wb_ref[0])
    merged = merged + g1_ref[...] * _dot(yb_ref[...], wb_ref[1])
    merged = merged + g2_ref[...] * _dot(yc_ref[...], wb_ref[2])
    xn = x_ref[...] + _dot(merged.astype(BF16), wo_ref[...])
    xo_ref[...] = xn
    ho_ref[...] = _rms(xn, gn_ref[...]).astype(ho_ref.dtype)


def _merge_call(ya, yb, yc, gates, x, wb, wo, layer, gn, tm=256):
    yblk, xblk = (tm, D_BR), (tm, D_MODEL)
    row_map = lambda i: (i, 0)
    in_specs = [
        pl.BlockSpec(yblk, row_map), pl.BlockSpec(yblk, row_map), pl.BlockSpec(yblk, row_map),
        pl.BlockSpec(xblk, lambda i: (i, 0)),
        pl.BlockSpec(xblk, lambda i: (i, 1)),
        pl.BlockSpec(xblk, lambda i: (i, 2)),
        pl.BlockSpec(xblk, row_map),
        _resident((None, N_BRANCH, D_BR, D_MODEL), lambda i: (layer, 0, 0, 0)),
        _resident((None, D_MODEL, D_MODEL), lambda i: (layer, 0, 0)),
        pl.BlockSpec((1, D_MODEL), lambda i: (0, 0)),
    ]
    return pl.pallas_call(
        _merge_kernel,
        out_shape=(jax.ShapeDtypeStruct((T, D_MODEL), F32),
                   jax.ShapeDtypeStruct((T, D_MODEL), BF16)),
        grid=(T // tm,),
        in_specs=in_specs,
        out_specs=(pl.BlockSpec(xblk, row_map), pl.BlockSpec(xblk, row_map)),
        compiler_params=_params(
            ("parallel",),
            [(yblk, BF16)] * 3 + [(xblk, gates.dtype)] * 3 + [(xblk, F32)] * 2 + [(xblk, BF16)],
            resident=_nbytes((N_BRANCH, D_BR, D_MODEL), BF16) + _nbytes((D_MODEL, D_MODEL), BF16),
            temps=3 * _nbytes(xblk, F32)),
        name="merge_out",
    )(ya, yb, yc, gates, gates, gates, x, wb, wo, gn)


def _memkv_kernel(m_ref, g_ref, wk_ref, wv_ref, k32_ref, v32_ref, k16_ref, v16_ref):
    m = _rms(m_ref[...], g_ref[...]).astype(BF16)
    stride = HEAD_LANE_TILES * X_HEADS
    for w_ref, o32_ref, o16_ref in ((wk_ref, k32_ref, k16_ref), (wv_ref, v32_ref, v16_ref)):
        z = _dot(m, w_ref[...])
        o16_ref[...] = z.astype(BF16)
        for h in range(X_HEADS):
            for t in range(HEAD_LANE_TILES):
                c0 = h * X_HEAD_DIM + t * LANES
                o32_ref[pl.ds(t * X_HEADS + h, MEM_LEN, stride=stride), :] = z[:, c0:c0 + LANES]


def _memkv_call(mem, g, wk, wv):
    mblk, wblk = (MEM_LEN, D_MODEL), (None, D_MODEL, D_MODEL)
    o32 = jax.ShapeDtypeStruct((DEPTH, BATCH, KV_ROWS, LANES), F32)
    o16 = jax.ShapeDtypeStruct((DEPTH, BATCH * MEM_LEN, D_MODEL), BF16)
    o32_spec = pl.BlockSpec((None, None, KV_ROWS, LANES), lambda l, b: (l, b, 0, 0))
    o16_spec = pl.BlockSpec((None,) + mblk, lambda l, b: (l, b, 0))
    return pl.pallas_call(
        _memkv_kernel,
        out_shape=(o32, o32, o16, o16),
        grid=(DEPTH, BATCH),
        in_specs=[pl.BlockSpec(mblk, lambda l, b: (b, 0)),
                  pl.BlockSpec((None, 1, D_MODEL), lambda l, b: (l, 0, 0)),
                  pl.BlockSpec(wblk, lambda l, b: (l, 0, 0)),
                  pl.BlockSpec(wblk, lambda l, b: (l, 0, 0))],
        out_specs=(o32_spec, o32_spec, o16_spec, o16_spec),
        compiler_params=_params(
            ("parallel", "parallel"),
            [((D_MODEL, D_MODEL), BF16)] * 2 + [(mblk, F32)] * 3 + [(mblk, BF16)] * 2,
            temps=4 * _nbytes(mblk, F32)),
        name="mem_kv",
    )(mem, g, wk, wv)


def _rows_to_cache(c):
    d, b = c.shape[:2]
    c = c.reshape(d, b, MEM_LEN, HEAD_LANE_TILES, X_HEADS, LANES).transpose(0, 1, 2, 4, 3, 5)
    return c.reshape(d, b, MEM_LEN, X_HEADS, X_HEAD_DIM)


_NT = (((1,), (1,)), ((), ()))


def _softmax_rows(s):
    e = jnp.exp(s - jnp.max(s, axis=-1, keepdims=True))
    return e * (1.0 / jnp.sum(e, axis=-1, keepdims=True))


def _xattn_prompt_kernel(n_prompt, q_ref, k_ref, v_ref, o_ref):
    scale = X_HEAD_DIM ** -0.5
    is_prompt = pl.program_id(0) < n_prompt

    @pl.when(is_prompt)
    def _():
        for h in range(X_HEADS):
            hs = slice(h * X_HEAD_DIM, (h + 1) * X_HEAD_DIM)
            s = lax.dot_general(q_ref[:, hs], k_ref[:, hs], _NT,
                                preferred_element_type=F32) * scale
            pr = _softmax_rows(s).astype(BF16)
            o_ref[:, hs] = _dot(pr, v_ref[:, hs]).astype(o_ref.dtype)

    @pl.when(jnp.logical_not(is_prompt))
    def _():
        o_ref[...] = jnp.zeros_like(o_ref)


def _xattn_prompt_call(q, k16, v16, layer, tq=512):
    qblk, kblk = (tq, D_MODEL), (None, MEM_LEN, D_MODEL)
    per_seq = SEQ // tq
    n_prompt = T_P // tq
    batch_map = lambda i: (layer, jnp.minimum(i // per_seq, BATCH - 1), 0)
    return pl.pallas_call(
        functools.partial(_xattn_prompt_kernel, n_prompt),
        out_shape=jax.ShapeDtypeStruct((T, D_MODEL), BF16),
        grid=(T // tq,),
        in_specs=[pl.BlockSpec(qblk, lambda i: (i, 0)),
                  pl.BlockSpec(kblk, batch_map),
                  pl.BlockSpec(kblk, batch_map)],
        out_specs=pl.BlockSpec(qblk, lambda i: (i, 0)),
        compiler_params=_params(("parallel",),
                                [(qblk, BF16)] * 2 + [((MEM_LEN, D_MODEL), BF16)] * 2,
                                temps=4 * _nbytes((tq, MEM_LEN), F32) + _nbytes(qblk, F32)),
        name="xattn_prompt",
    )(q, k16, v16)


LANES = 128
HEAD_LANE_TILES = X_HEAD_DIM // LANES
KV_ROWS = MEM_LEN * HEAD_LANE_TILES * X_HEADS
KV_BB = 4


def _cache_rows(c):
    d, b = c.shape[:2]
    c = c.reshape(d, b, MEM_LEN, X_HEADS, HEAD_LANE_TILES, LANES).transpose(0, 1, 2, 4, 3, 5)
    return c.reshape(d, b, KV_ROWS, LANES)


def _head_rows(ref, i, h):
    stride = HEAD_LANE_TILES * X_HEADS
    tiles = [ref[i, pl.ds(t * X_HEADS + h, MEM_LEN, stride=stride), :]
             for t in range(HEAD_LANE_TILES)]
    return jnp.concatenate(tiles, axis=1)


def _xattn_sample_kernel(q_ref, k_ref, v_ref, o_in, o_ref, acc):
    del o_in
    c = pl.program_id(1)
    scale = X_HEAD_DIM ** -0.5

    @pl.when(c == 0)
    def _():
        acc[...] = jnp.zeros_like(acc)

    owner = lax.broadcasted_iota(jnp.int32, (SBLK, MEM_LEN), 0) % SB
    for h in range(X_HEADS):
        hs = slice(h * X_HEAD_DIM, (h + 1) * X_HEAD_DIM)
        keys = jnp.concatenate([_head_rows(k_ref, ii, h) for ii in range(KV_BB)], axis=0)
        s = lax.dot_general(q_ref[:, hs], keys.astype(BF16), _NT,
                            preferred_element_type=F32) * scale
        pr = [jnp.where(owner == c * KV_BB + ii,
                        _softmax_rows(s[:, ii * MEM_LEN:(ii + 1) * MEM_LEN]), 0.0)
              for ii in range(KV_BB)]
        vals = jnp.concatenate([_head_rows(v_ref, ii, h) for ii in range(KV_BB)], axis=0)
        acc[:, hs] += _dot(jnp.concatenate(pr, axis=1).astype(BF16), vals.astype(BF16))

    @pl.when(c == pl.num_programs(1) - 1)
    def _():
        o_ref[...] = acc[...].astype(o_ref.dtype)


def _xattn_sample_call(q, mk, mv, layer, o):
    qblk = (SBLK, D_MODEL)
    kblk = (None, KV_BB, KV_ROWS, LANES)
    blk0 = T_P // SBLK
    per_blk = SB // KV_BB
    return pl.pallas_call(
        _xattn_sample_kernel,
        out_shape=jax.ShapeDtypeStruct((T, D_MODEL), BF16),
        grid=(N_SBLK, per_blk),
        in_specs=[pl.BlockSpec(qblk, lambda j, c: (blk0 + j, 0)),
                  pl.BlockSpec(kblk, lambda j, c: (layer, j * per_blk + c, 0, 0)),
                  pl.BlockSpec(kblk, lambda j, c: (layer, j * per_blk + c, 0, 0)),
                  pl.BlockSpec(memory_space=pl.ANY)],
        out_specs=pl.BlockSpec(qblk, lambda j, c: (blk0 + j, 0)),
        scratch_shapes=[pltpu.VMEM(qblk, F32)],
        input_output_aliases={3: 0},
        compiler_params=_params(("parallel", "arbitrary"),
                                [(qblk, BF16)] * 2 + [((KV_BB, KV_ROWS, LANES), F32)] * 2,
                                resident=_nbytes(qblk, F32),
                                temps=4 * KV_BB * _nbytes((MEM_LEN, X_HEAD_DIM), F32)),
        name="xattn_sample",
    )(q, mk, mv, o)


def _resid_kernel(a_ref, w_ref, x_ref, gn_ref, xo_ref, ho_ref):
    xn = x_ref[...] + _dot(a_ref[...], w_ref[...])
    xo_ref[...] = xn
    ho_ref[...] = _rms(xn, gn_ref[...]).astype(ho_ref.dtype)


def _resid_call(a, w, layer, x, gn, tm=512):
    blk = (tm, D_MODEL)
    row_map = lambda i: (i, 0)
    return pl.pallas_call(
        _resid_kernel,
        out_shape=(jax.ShapeDtypeStruct((T, D_MODEL), F32),
                   jax.ShapeDtypeStruct((T, D_MODEL), BF16)),
        grid=(T // tm,),
        in_specs=[pl.BlockSpec(blk, row_map),
                  _resident((None, D_MODEL, D_MODEL), lambda i: (layer, 0, 0)),
                  pl.BlockSpec(blk, row_map),
                  pl.BlockSpec((1, D_MODEL), lambda i: (0, 0))],
        out_specs=(pl.BlockSpec(blk, row_map), pl.BlockSpec(blk, row_map)),
        compiler_params=_params(("parallel",),
                                [(blk, BF16)] * 2 + [(blk, F32)] * 2,
                                resident=_nbytes((D_MODEL, D_MODEL), BF16),
                                temps=2 * _nbytes(blk, F32)),
        name="attn_out",
    )(a, w, x, gn)


def _mlp_kernel(n_prompt, h_ref, wu_ref, wd_ref, x_ref, gn_ref, o0_ref, o1_ref, acc):
    i, f = pl.program_id(0), pl.program_id(1)

    @pl.when(f == 0)
    def _():
        acc[...] = x_ref[...]

    hid = jnp.maximum(_dot(h_ref[...], wu_ref[...]), 0.0)
    acc[...] += _dot((hid * hid).astype(BF16), wd_ref[...])

    @pl.when(f == pl.num_programs(1) - 1)
    def _():
        xn = acc[...]
        if n_prompt is None:
            o0_ref[...] = xn
            o1_ref[...] = _rms(xn, gn_ref[...]).astype(o1_ref.dtype)
        else:
            y = _rms(xn, gn_ref[...])

            @pl.when(i < n_prompt)
            def _():
                o0_ref[...] = y

            @pl.when(i >= n_prompt)
            def _():
                o1_ref[...] = y


def _mlp_call(h, wu, wd, layer, x, gn, final, tm=512, tf=1024):
    blk = (tm, D_MODEL)
    row_map = lambda i, f: (i, 0)
    if final:
        n_prompt = T_P // tm
        out_shape = (jax.ShapeDtypeStruct((T_P, D_MODEL), F32),
                     jax.ShapeDtypeStruct((T_S, D_MODEL), F32))
        out_specs = (pl.BlockSpec(blk, lambda i, f: (jnp.minimum(i, n_prompt - 1), 0)),
                     pl.BlockSpec(blk, lambda i, f: (jnp.maximum(i - n_prompt, 0), 0)))
        out_dtype = F32
    else:
        n_prompt = None
        out_shape = (jax.ShapeDtypeStruct((T, D_MODEL), F32),
                     jax.ShapeDtypeStruct((T, D_MODEL), BF16))
        out_specs = (pl.BlockSpec(blk, row_map), pl.BlockSpec(blk, row_map))
        out_dtype = BF16
    return pl.pallas_call(
        functools.partial(_mlp_kernel, n_prompt),
        out_shape=out_shape,
        grid=(T // tm, D_FF // tf),
        in_specs=[pl.BlockSpec(blk, row_map),
                  pl.BlockSpec((None, D_MODEL, tf), lambda i, f: (layer, 0, f)),
                  pl.BlockSpec((None, tf, D_MODEL), lambda i, f: (layer, f, 0)),
                  pl.BlockSpec(blk, row_map),
                  pl.BlockSpec((1, D_MODEL), lambda i, f: (0, 0))],
        out_specs=out_specs,
        scratch_shapes=[pltpu.VMEM(blk, F32)],
        compiler_params=_params(
            ("arbitrary", "arbitrary"),
            [(blk, BF16), ((D_MODEL, tf), BF16), ((tf, D_MODEL), BF16), (blk, F32), (blk, F32),
             (blk, out_dtype)],
            resident=_nbytes(blk, F32),
            temps=2 * _nbytes((tm, tf), F32) + _nbytes(blk, F32)),
        name="mlp_final" if final else "mlp",
    )(h, wu, wd, x, gn)


def _sample_to_rows(a):
    c = a.shape[-1]
    return a.reshape(N_SBLK, SB, DEC_SEQ, c).transpose(0, 2, 1, 3).reshape(T_S, c)


def _rows_to_sample(a):
    c = a.shape[-1]
    return a.reshape(N_SBLK, DEC_SEQ, SB, c).transpose(0, 2, 1, 3).reshape(DEC_BATCH, DEC_SEQ, c)


def kernel(x_prompt, x_sample, mem_prompt, cache_mem_k, cache_mem_v, state_conv, state_pool,
           g_mix, w_in, ln_v_g, ln_v_b, w_s, b_s, conv_w, conv_b, ln_c_g, ln_c_b,
           pool_w, pool_scale, w_branch, w_out, g_xattn, g_mem, w_xq, w_xk, w_xv, w_xo,
           g_mlp, w_up, w_down, g_final):
    assert T_S == 512 and T_P % 512 == 0
    row = lambda a: a.reshape(1, -1)
    mem = mem_prompt.reshape(BATCH * MEM_LEN, D_MODEL)
    cache_k, cache_v = _cache_rows(cache_mem_k), _cache_rows(cache_mem_v)
    wb16, wo16 = w_branch.astype(BF16), w_out.astype(BF16)
    wk16, wv16, wxo16 = w_xk.astype(BF16), w_xv.astype(BF16), w_xo.astype(BF16)
    wu16, wd16, pw16 = w_up.astype(BF16), w_down.astype(BF16), pool_w.astype(BF16)
    n1 = D_BR // 1024

    x, h = _norm_call(x_prompt.reshape(T_P, D_MODEL), _sample_to_rows(x_sample), row(g_mix[0]))
    k32, v32, k16, v16 = _memkv_call(mem, g_mem.reshape(DEPTH, 1, D_MODEL), wk16, wv16)
    conv_p, pool_p, conv_s, pool_s, chunk_s = [], [], [], [], []
    y_prompt = y_sample = None
    for l in range(DEPTH):
        ug = _proj_call(h, w_in, l, col0=0, ncols=n1, mode="gelu", out_dtype=F32, name="proj_u")
        vn = _proj_call(h, w_in, l, col0=n1, ncols=n1, mode="gelu_ln", out_dtype=F32,
                        name="proj_v", extra=(row(ln_v_g[l]), row(ln_v_b[l])))
        glu = _proj_call(h, w_in, l, col0=2 * D_A // 512, ncols=D_CONV // 512, mode="glu",
                         out_dtype=F32, name="proj_glu", tn=512)
        p = _proj_call(h, w_in, l, col0=4 * n1, ncols=n1, mode="id", out_dtype=F32,
                       name="proj_p")
        gates = _proj_call(h, w_in, l, col0=5 * n1, ncols=N_BRANCH * D_MODEL // 1024,
                           mode="sigmoid", out_dtype=BF16, name="proj_gate")

        cw, cb = conv_w[l], row(conv_b[l])
        lg, lb = row(ln_c_g[l]), row(ln_c_b[l])
        pw, psc = pw16[l], row(pool_scale[l])
        bs_full = jnp.repeat(b_s[l].T, DH_A, axis=1)
        ya, yb, yc = _mix_prompt_call(ug, vn, glu, p, w_s[l], bs_full, cw, cb, lg, lb, pw, psc)
        wsm = jnp.repeat(w_s[l][:, :DEC_SEQ, :DEC_SEQ].transpose(1, 2, 0), DH_A, axis=2)
        bsm = jnp.repeat(b_s[l][:, :DEC_SEQ].T, DH_A, axis=1)
        cpast = state_conv[l].transpose(1, 0, 2)
        ppast = state_pool[l].transpose(1, 0, 2)
        ya, yb, yc = _mix_sample_call(ug, vn, glu, p, cpast, ppast, wsm, bsm, cw, cb, lg, lb,
                                      pw, psc, ya, yb, yc)

        x, hx = _merge_call(ya, yb, yc, gates, x, wb16, wo16, l, row(g_xattn[l]))

        q = _proj_call(hx, w_xq, l, col0=0, ncols=D_MODEL // 1024, mode="id", out_dtype=BF16,
                       name="proj_q")
        o = _xattn_prompt_call(q, k16, v16, l)
        o = _xattn_sample_call(q, cache_k, cache_v, l, o)
        x, hm = _resid_call(o, wxo16, l, x, row(g_mlp[l]))

        if l == DEPTH - 1:
            y_prompt, y_sample = _mlp_call(hm, wu16, wd16, l, x, row(g_final), True)
        else:
            x, h = _mlp_call(hm, wu16, wd16, l, x, row(g_mix[l + 1]), False)

        conv_p.append(jnp.stack([glu[(b + 1) * SEQ - (CONV_K - 1):(b + 1) * SEQ]
                                 for b in range(BATCH)]))
        pool_p.append(jnp.stack([p[(b + 1) * SEQ - POOL_PAST:(b + 1) * SEQ]
                                 for b in range(BATCH)]))
        conv_s.append(jnp.concatenate([state_conv[l][:, DEC_SEQ:], _rows_to_sample(glu[T_P:])],
                                      axis=1))
        pool_s.append(jnp.concatenate([state_pool[l][:, DEC_SEQ:], _rows_to_sample(p[T_P:])],
                                      axis=1))
        chunk_s.append(_rows_to_sample(vn[T_P:]))

    y_prompt = y_prompt.reshape(BATCH, SEQ, D_MODEL)
    y_sample = _rows_to_sample(y_sample)
    return (y_prompt, y_sample, _rows_to_cache(k32), _rows_to_cache(v32), jnp.stack(conv_p),
            jnp.stack(pool_p), jnp.stack(conv_s), jnp.stack(pool_s), jnp.stack(chunk_s))
```

```python
import functools

import jax
import jax.numpy as jnp
from jax import lax
from jax.experimental import pallas as pl
from jax.experimental.pallas import tpu as pltpu

D_MODEL = 2048
BATCH = 4
SEQ = 2048
DEPTH = 2
DEC_BATCH = 128
DEC_SEQ = 4
PAST_LEN = 16384
D_BR = D_MODEL // 2
D_A = D_BR
H_A = 4
DH_A = D_A // H_A
CHUNK = 128
D_CONV = D_BR
CONV_K = 31
D_POOL = D_BR
POOL_WINDOWS = (2, 4, 8, 16)
N_POOL = len(POOL_WINDOWS)
G_POOL = D_POOL // N_POOL
POOL_PAST = max(POOL_WINDOWS) - 1
MEM_LEN = 256
X_HEADS = 4
X_HEAD_DIM = D_MODEL // X_HEADS
D_FF = 4 * D_MODEL
N_BRANCH = 3
D_IN = 2 * D_A + 2 * D_CONV + D_POOL + N_BRANCH * D_MODEL
RMS_EPS = 1e-6
LN_EPS = 1e-5

T_P = BATCH * SEQ
T_S = DEC_BATCH * DEC_SEQ
T = T_P + T_S
SB = 16
SBLK = DEC_SEQ * SB
N_SBLK = DEC_BATCH // SB

SUBLANES = 8
VMEM_PHYSICAL_BYTES = 64 * 1024 * 1024
VMEM_CEILING_BYTES = VMEM_PHYSICAL_BYTES - 6 * 1024 * 1024

F32 = jnp.float32
BF16 = jnp.bfloat16


def _nbytes(shape, dtype):
    n = 1
    for s in shape:
        n *= s
    return n * jnp.dtype(dtype).itemsize


def _params(semantics, pipelined, resident=0, temps=0):
    need = 2 * sum(_nbytes(s, d) for s, d in pipelined) + resident + temps
    need = need + need // 8 + (2 << 20)
    return pltpu.CompilerParams(
        dimension_semantics=semantics,
        vmem_limit_bytes=int(min(max(need, 16 << 20), VMEM_CEILING_BYTES)))


def _resident(shape, index_map):
    return pl.BlockSpec(shape, index_map, pipeline_mode=pl.Buffered(1))


def _layer_spec(shape, layer):
    zeros = (0,) * len(shape)
    return pl.BlockSpec((None,) + tuple(shape), lambda *_: (layer,) + zeros)


def _sigmoid(x):
    return 0.5 * jnp.tanh(0.5 * x) + 0.5


def _rms(x, g):
    return x * lax.rsqrt(jnp.mean(x * x, axis=-1, keepdims=True) + RMS_EPS) * g


def _ln(x, g, b):
    mu = jnp.mean(x, axis=-1, keepdims=True)
    xc = x - mu
    var = jnp.mean(xc * xc, axis=-1, keepdims=True)
    return xc * lax.rsqrt(var + LN_EPS) * g + b


def _dot(a, b):
    return jnp.dot(a, b, preferred_element_type=F32)


def _norm_kernel(n_prompt, xp_ref, xs_ref, g_ref, x_ref, h_ref):
    def emit(src):
        x = src[...]
        x_ref[...] = x
        h_ref[...] = _rms(x, g_ref[...]).astype(h_ref.dtype)

    @pl.when(pl.program_id(0) < n_prompt)
    def _():
        emit(xp_ref)

    @pl.when(pl.program_id(0) >= n_prompt)
    def _():
        emit(xs_ref)


def _norm_call(xp, xs, g, tm=512):
    n_prompt = T_P // tm
    blk = (tm, D_MODEL)
    return pl.pallas_call(
        functools.partial(_norm_kernel, n_prompt),
        out_shape=(jax.ShapeDtypeStruct((T, D_MODEL), F32),
                   jax.ShapeDtypeStruct((T, D_MODEL), BF16)),
        grid=(T // tm,),
        in_specs=[pl.BlockSpec(blk, lambda i: (jnp.minimum(i, n_prompt - 1), 0)),
                  pl.BlockSpec(blk, lambda i: (jnp.maximum(i - n_prompt, 0), 0)),
                  _layer_spec((1, D_MODEL), 0)],
        out_specs=(pl.BlockSpec(blk, lambda i: (i, 0)), pl.BlockSpec(blk, lambda i: (i, 0))),
        compiler_params=_params(("arbitrary",), [(blk, F32)] * 3 + [(blk, BF16)],
                                temps=2 * _nbytes(blk, F32)),
        name="rmsnorm_in",
    )(xp, xs, g)


def _proj_kernel(mode, *refs):
    n_w = 2 if mode == "glu" else 1
    n_extra = 2 if mode == "gelu_ln" else 0
    h_ref, w_refs = refs[0], refs[1:1 + n_w]
    extra = refs[1 + n_w:1 + n_w + n_extra]
    o_ref = refs[1 + n_w + n_extra]
    wbf = refs[2 + n_w + n_extra:]

    @pl.when(pl.program_id(1) == 0)
    def _():
        for src, dst in zip(w_refs, wbf):
            dst[...] = src[...].astype(BF16)

    h = h_ref[...]
    z = _dot(h, wbf[0][...])
    if mode == "glu":
        z = z * _sigmoid(_dot(h, wbf[1][...]))
    elif mode == "gelu":
        z = jax.nn.gelu(z)
    elif mode == "gelu_ln":
        z = _ln(jax.nn.gelu(z), extra[0][...], extra[1][...])
    elif mode == "sigmoid":
        z = _sigmoid(z)
    o_ref[...] = z.astype(o_ref.dtype)


def _proj_call(h, w, layer, *, col0, ncols, mode, out_dtype, name, extra=(), tm=1088, tn=1024):
    n, k = h.shape
    hblk, wblk, oblk = (tm, k), (k, tn), (tm, tn)
    n_w = 2 if mode == "glu" else 1
    in_specs = [pl.BlockSpec(hblk, lambda j, i: (i, 0)),
                pl.BlockSpec((None,) + wblk, lambda j, i: (layer, 0, col0 + j))]
    args = [h, w]
    if mode == "glu":
        in_specs.append(pl.BlockSpec((None,) + wblk, lambda j, i: (layer, 0, col0 + ncols + j)))
        args.append(w)
    for e in extra:
        in_specs.append(_layer_spec((1, tn), layer))
        args.append(e)
    return pl.pallas_call(
        functools.partial(_proj_kernel, mode),
        out_shape=jax.ShapeDtypeStruct((n, ncols * tn), out_dtype),
        grid=(ncols, n // tm),
        in_specs=in_specs,
        out_specs=pl.BlockSpec(oblk, lambda j, i: (i, j)),
        scratch_shapes=[pltpu.VMEM(wblk, BF16)] * n_w,
        compiler_params=_params(("arbitrary", "arbitrary"),
                                [(hblk, h.dtype), (oblk, out_dtype)] + [(wblk, F32)] * n_w,
                                resident=n_w * _nbytes(wblk, BF16),
                                temps=(2 + n_w) * _nbytes(oblk, F32)),
        name=name,
    )(*args)


MIX_TM = 256
MIX_RB = 32
CONV_HALO = 32
POOL_HALO = 16


def _conv_ln_silu(window, cw_ref, cb, lg, lb):
    acc = cb + cw_ref[0:1, :] * window(0)
    for k in range(1, CONV_K):
        acc = acc + cw_ref[k:k + 1, :] * window(k)
    y = _ln(acc, lg, lb)
    return y * _sigmoid(y)


def _mix_prompt_kernel(*refs):
    outs = refs[14:17]
    is_prompt = pl.program_id(0) < T_P // MIX_TM

    @pl.when(is_prompt)
    def _():
        _mix_prompt_tile(*refs)

    @pl.when(jnp.logical_not(is_prompt))
    def _():
        for o_ref in outs:
            o_ref[...] = jnp.zeros_like(o_ref)


def _mix_prompt_tile(ug_ref, vn_ref, glu_ref, gh_ref, p_ref, ph_ref,
                     ws_ref, bs_ref, cw_ref, cb_ref, lg_ref, lb_ref, pw_ref, psc_ref,
                     ya_ref, yb_ref, yc_ref, gwin, gshift, pwin):
    tm = MIX_TM
    seq_tile = pl.program_id(0) % (SEQ // tm)
    first = seq_tile == 0

    row = lax.broadcasted_iota(jnp.int32, (CHUNK, CHUNK), 0)
    col = lax.broadcasted_iota(jnp.int32, (CHUNK, CHUNK), 1)
    for h in range(H_A):
        w = jnp.where(row >= col, ws_ref[h], 0.0).astype(BF16)
        hs = slice(h * DH_A, (h + 1) * DH_A)
        for c in range(tm // CHUNK):
            rs = slice(c * CHUNK, (c + 1) * CHUNK)
            mixed = _dot(w, vn_ref[rs, hs].astype(BF16)) + bs_ref[:, hs]
            ya_ref[rs, hs] = (ug_ref[rs, hs] * mixed).astype(ya_ref.dtype)

    gwin[0:CONV_HALO, :] = jnp.where(first, 0.0, gh_ref[...])
    gwin[CONV_HALO:, :] = glu_ref[...]
    n_shift = CONV_HALO + tm - SUBLANES
    for r in range(1, SUBLANES):
        gshift[r - 1, 0:n_shift, :] = gwin[r:r + n_shift, :]
    cb, lg, lb = cb_ref[...], lg_ref[...], lb_ref[...]
    off = CONV_HALO - (CONV_K - 1)

    def window(r0, k):
        a, r = divmod(off + k, SUBLANES)
        lo = r0 + SUBLANES * a
        if r == 0:
            return gwin[lo:lo + MIX_RB, :]
        return gshift[r - 1, lo:lo + MIX_RB, :]

    for rb in range(tm // MIX_RB):
        r0 = rb * MIX_RB
        y = _conv_ln_silu(functools.partial(window, r0), cw_ref, cb, lg, lb)
        yb_ref[r0:r0 + MIX_RB, :] = y.astype(yb_ref.dtype)

    pwin[0:POOL_HALO, :] = jnp.where(first, 0.0, ph_ref[...])
    pwin[POOL_HALO:, :] = p_ref[...]
    pos = seq_tile * tm + lax.broadcasted_iota(jnp.int32, (tm, 1), 0)
    for g, wlen in enumerate(POOL_WINDOWS):
        gs = slice(g * G_POOL, (g + 1) * G_POOL)
        s = pwin[POOL_HALO:POOL_HALO + tm, gs]
        for d in range(1, wlen):
            s = s + pwin[POOL_HALO - d:POOL_HALO - d + tm, gs]
        inv = 1.0 / jnp.minimum(wlen, pos + 1).astype(F32)
        mixed = s * inv - p_ref[:, gs]
        y = _dot(mixed.astype(BF16), pw_ref[g]) * psc_ref[:, gs]
        yc_ref[:, gs] = y.astype(yc_ref.dtype)


def _mix_prompt_call(ug, vn, glu, p, ws, bs_full, cw, cb, lg, lb, pw, psc, layer):
    tm = MIX_TM
    blk = (tm, D_BR)
    last = T_P // tm - 1
    row_map = lambda i: (i, 0)
    in_map = lambda i: (jnp.minimum(i, last), 0)
    in_specs = [
        pl.BlockSpec(blk, in_map),
        pl.BlockSpec(blk, in_map),
        pl.BlockSpec(blk, in_map),
        pl.BlockSpec((CONV_HALO, D_BR), lambda i: (
            jnp.maximum(jnp.minimum(i, last) * (tm // CONV_HALO) - 1, 0), 0)),
        pl.BlockSpec(blk, in_map),
        pl.BlockSpec((POOL_HALO, D_BR), lambda i: (
            jnp.maximum(jnp.minimum(i, last) * (tm // POOL_HALO) - 1, 0), 0)),
        _layer_spec((H_A, CHUNK, CHUNK), layer),
        _layer_spec((CHUNK, D_A), layer),
        _layer_spec((CONV_K, D_CONV), layer),
        _layer_spec((1, D_CONV), layer),
        _layer_spec((1, D_CONV), layer),
        _layer_spec((1, D_CONV), layer),
        _layer_spec((N_POOL, G_POOL, G_POOL), layer),
        _layer_spec((1, D_POOL), layer),
    ]
    out = jax.ShapeDtypeStruct((T, D_BR), BF16)
    return pl.pallas_call(
        _mix_prompt_kernel,
        out_shape=(out, out, out),
        grid=(T // tm,),
        in_specs=in_specs,
        out_specs=(pl.BlockSpec(blk, row_map),) * 3,
        scratch_shapes=[pltpu.VMEM((CONV_HALO + tm, D_CONV), F32),
                        pltpu.VMEM((SUBLANES - 1, CONV_HALO + tm, D_CONV), F32),
                        pltpu.VMEM((POOL_HALO + tm, D_POOL), F32)],
        compiler_params=_params(
            ("parallel",),
            [(blk, F32)] * 4 + [(blk, BF16)] * 3 + [((CONV_HALO + POOL_HALO, D_BR), F32),
                                                    ((CHUNK + CONV_K + 8, D_BR), F32)],
            resident=_nbytes((SUBLANES * (CONV_HALO + tm) + POOL_HALO + tm, D_BR), F32),
            temps=20 * _nbytes(blk, F32)),
        name="mix_prompt",
    )(ug, vn, glu, glu, p, p, ws, bs_full, cw, cb, lg, lb, pw, psc)


def _mix_sample_kernel(ug_ref, vn_ref, glu_ref, p_ref, cpast_ref, ppast_ref,
                       wsm_ref, bsm_ref, cw_ref, cb_ref, lg_ref, lb_ref, pw_ref, psc_ref,
                       ya_in, yb_in, yc_in, ya_ref, yb_ref, yc_ref, mix_scr):
    del ya_in, yb_in, yc_in
    cb, lg, lb = cb_ref[...], lg_ref[...], lb_ref[...]

    def rows(t):
        return slice(t * SB, (t + 1) * SB)

    def conv_src(j):
        return cpast_ref[j] if j < CONV_K - 1 else glu_ref[rows(j - (CONV_K - 1)), :]

    def pool_src(j, gs):
        return ppast_ref[j, :, gs] if j < POOL_PAST else p_ref[rows(j - POOL_PAST), gs]

    for t in range(DEC_SEQ):
        m = bsm_ref[t:t + 1, :] + wsm_ref[t, 0:1, :] * vn_ref[rows(0), :]
        for s in range(1, t + 1):
            m = m + wsm_ref[t, s:s + 1, :] * vn_ref[rows(s), :]
        ya_ref[rows(t), :] = (ug_ref[rows(t), :] * m).astype(ya_ref.dtype)
        y = _conv_ln_silu(lambda k: conv_src(t + k), cw_ref, cb, lg, lb)
        yb_ref[rows(t), :] = y.astype(yb_ref.dtype)
        for g, wlen in enumerate(POOL_WINDOWS):
            gs = slice(g * G_POOL, (g + 1) * G_POOL)
            s = pool_src(POOL_PAST + t, gs)
            for d in range(1, wlen):
                s = s + pool_src(POOL_PAST + t - d, gs)
            cnt = float(min(wlen, PAST_LEN + t + 1))
            mix_scr[rows(t), gs] = s / cnt - p_ref[rows(t), gs]
    for g in range(N_POOL):
        gs = slice(g * G_POOL, (g + 1) * G_POOL)
        y = _dot(mix_scr[:, gs].astype(BF16), pw_ref[g]) * psc_ref[:, gs]
        yc_ref[:, gs] = y.astype(yc_ref.dtype)


def _mix_sample_call(ug, vn, glu, p, cpast, ppast, wsm, bsm, cw, cb, lg, lb, pw, psc, layer,
                     ya, yb, yc):
    blk = (SBLK, D_BR)
    blk0 = T_P // SBLK
    row_map = lambda j: (blk0 + j, 0)
    any_spec = pl.BlockSpec(memory_space=pl.ANY)
    in_specs = [
        pl.BlockSpec(blk, row_map), pl.BlockSpec(blk, row_map),
        pl.BlockSpec(blk, row_map), pl.BlockSpec(blk, row_map),
        pl.BlockSpec((None, CONV_K - 1, SB, D_CONV), lambda j: (layer, 0, j, 0)),
        pl.BlockSpec((None, POOL_PAST, SB, D_POOL), lambda j: (layer, 0, j, 0)),
        _layer_spec((DEC_SEQ, DEC_SEQ, D_A), layer),
        _layer_spec((DEC_SEQ, D_A), layer),
        _layer_spec((CONV_K, D_CONV), layer),
        _layer_spec((1, D_CONV), layer),
        _layer_spec((1, D_CONV), layer),
        _layer_spec((1, D_CONV), layer),
        _layer_spec((N_POOL, G_POOL, G_POOL), layer),
        _layer_spec((1, D_POOL), layer),
        any_spec, any_spec, any_spec,
    ]
    out = jax.ShapeDtypeStruct((T, D_BR), BF16)
    return pl.pallas_call(
        _mix_sample_kernel,
        out_shape=(out, out, out),
        grid=(N_SBLK,),
        in_specs=in_specs,
        out_specs=(pl.BlockSpec(blk, row_map),) * 3,
        scratch_shapes=[pltpu.VMEM(blk, F32)],
        input_output_aliases={14: 0, 15: 1, 16: 2},
        compiler_params=_params(
            ("parallel",),
            [(blk, F32)] * 4 + [(blk, BF16)] * 3
            + [((CONV_K - 1 + POOL_PAST, SB, D_BR), F32), ((CONV_K + 16, D_BR), F32)],
            resident=_nbytes(blk, F32), temps=32 * _nbytes(blk, F32)),
        name="mix_sample",
    )(ug, vn, glu, p, cpast, ppast, wsm, bsm, cw, cb, lg, lb, pw, psc, ya, yb, yc)


def _merge_kernel(ya_ref, yb_ref, yc_ref, g0_ref, g1_ref, g2_ref, x_ref,
                  wb_ref, wo_ref, gn_ref, xo_ref, ho_ref):
    merged = g0_ref[...] * _dot(ya_ref[...], wb_ref[0])
    merged = merged + g1_ref[...] * _dot(yb_ref[...], wb_ref[1])
    merged = merged + g2_ref[...] * _dot(yc_ref[...], wb_ref[2])
    xn = x_ref[...] + _dot(merged.astype(BF16), wo_ref[...])
    xo_ref[...] = xn
    ho_ref[...] = _rms(xn, gn_ref[...]).astype(ho_ref.dtype)


def _merge_call(ya, yb, yc, gates, x, wb, wo, layer, gn, tm=256):
    yblk, xblk = (tm, D_BR), (tm, D_MODEL)
    row_map = lambda i: (i, 0)
    in_specs = [
        pl.BlockSpec(yblk, row_map), pl.BlockSpec(yblk, row_map), pl.BlockSpec(yblk, row_map),
        pl.BlockSpec(xblk, lambda i: (i, 0)),
        pl.BlockSpec(xblk, lambda i: (i, 1)),
        pl.BlockSpec(xblk, lambda i: (i, 2)),
        pl.BlockSpec(xblk, row_map),
        _resident((None, N_BRANCH, D_BR, D_MODEL), lambda i: (layer, 0, 0, 0)),
        _resident((None, D_MODEL, D_MODEL), lambda i: (layer, 0, 0)),
        _layer_spec((1, D_MODEL), layer),
    ]
    return pl.pallas_call(
        _merge_kernel,
        out_shape=(jax.ShapeDtypeStruct((T, D_MODEL), F32),
                   jax.ShapeDtypeStruct((T, D_MODEL), BF16)),
        grid=(T // tm,),
        in_specs=in_specs,
        out_specs=(pl.BlockSpec(xblk, row_map), pl.BlockSpec(xblk, row_map)),
        compiler_params=_params(
            ("parallel",),
            [(yblk, BF16)] * 3 + [(xblk, gates.dtype)] * 3 + [(xblk, F32)] * 2 + [(xblk, BF16)],
            resident=_nbytes((N_BRANCH, D_BR, D_MODEL), BF16) + _nbytes((D_MODEL, D_MODEL), BF16),
            temps=3 * _nbytes(xblk, F32)),
        name="merge_out",
    )(ya, yb, yc, gates, gates, gates, x, wb, wo, gn)


def _memkv_kernel(m_ref, g_ref, wk_ref, wv_ref, k32_ref, v32_ref, k16_ref, v16_ref):
    m = _rms(m_ref[...], g_ref[...]).astype(BF16)
    stride = HEAD_LANE_TILES * X_HEADS
    for w_ref, o32_ref, o16_ref in ((wk_ref, k32_ref, k16_ref), (wv_ref, v32_ref, v16_ref)):
        z = _dot(m, w_ref[...])
        o16_ref[...] = z.astype(BF16)
        for h in range(X_HEADS):
            for t in range(HEAD_LANE_TILES):
                c0 = h * X_HEAD_DIM + t * LANES
                o32_ref[pl.ds(t * X_HEADS + h, MEM_LEN, stride=stride), :] = z[:, c0:c0 + LANES]


def _memkv_call(mem, g, wk, wv):
    mblk, wblk = (MEM_LEN, D_MODEL), (None, D_MODEL, D_MODEL)
    o32 = jax.ShapeDtypeStruct((DEPTH, BATCH, KV_ROWS, LANES), F32)
    o16 = jax.ShapeDtypeStruct((DEPTH, BATCH * MEM_LEN, D_MODEL), BF16)
    o32_spec = pl.BlockSpec((None, None, KV_ROWS, LANES), lambda l, b: (l, b, 0, 0))
    o16_spec = pl.BlockSpec((None,) + mblk, lambda l, b: (l, b, 0))
    return pl.pallas_call(
        _memkv_kernel,
        out_shape=(o32, o32, o16, o16),
        grid=(DEPTH, BATCH),
        in_specs=[pl.BlockSpec(mblk, lambda l, b: (b, 0)),
                  pl.BlockSpec((None, 1, D_MODEL), lambda l, b: (l, 0, 0)),
                  pl.BlockSpec(wblk, lambda l, b: (l, 0, 0)),
                  pl.BlockSpec(wblk, lambda l, b: (l, 0, 0))],
        out_specs=(o32_spec, o32_spec, o16_spec, o16_spec),
        compiler_params=_params(
            ("parallel", "parallel"),
            [((D_MODEL, D_MODEL), BF16)] * 2 + [(mblk, F32)] * 3 + [(mblk, BF16)] * 2,
            temps=4 * _nbytes(mblk, F32)),
        name="mem_kv",
    )(mem, g, wk, wv)


def _rows_to_cache(c):
    d, b = c.shape[:2]
    c = c.reshape(d, b, MEM_LEN, HEAD_LANE_TILES, X_HEADS, LANES).transpose(0, 1, 2, 4, 3, 5)
    return c.reshape(d, b, MEM_LEN, X_HEADS, X_HEAD_DIM)


_NT = (((1,), (1,)), ((), ()))


def _softmax_rows(s):
    e = jnp.exp(s - jnp.max(s, axis=-1, keepdims=True))
    return e * (1.0 / jnp.sum(e, axis=-1, keepdims=True))


def _xattn_prompt_kernel(n_prompt, q_ref, k_ref, v_ref, o_ref):
    scale = X_HEAD_DIM ** -0.5
    is_prompt = pl.program_id(0) < n_prompt

    @pl.when(is_prompt)
    def _():
        for h in range(X_HEADS):
            hs = slice(h * X_HEAD_DIM, (h + 1) * X_HEAD_DIM)
            s = lax.dot_general(q_ref[:, hs], k_ref[:, hs], _NT,
                                preferred_element_type=F32) * scale
            pr = _softmax_rows(s).astype(BF16)
            o_ref[:, hs] = _dot(pr, v_ref[:, hs]).astype(o_ref.dtype)

    @pl.when(jnp.logical_not(is_prompt))
    def _():
        o_ref[...] = jnp.zeros_like(o_ref)


def _xattn_prompt_call(q, k16, v16, layer, tq=512):
    qblk, kblk = (tq, D_MODEL), (None, MEM_LEN, D_MODEL)
    per_seq = SEQ // tq
    n_prompt = T_P // tq
    batch_map = lambda i: (layer, jnp.minimum(i // per_seq, BATCH - 1), 0)
    return pl.pallas_call(
        functools.partial(_xattn_prompt_kernel, n_prompt),
        out_shape=jax.ShapeDtypeStruct((T, D_MODEL), BF16),
        grid=(T // tq,),
        in_specs=[pl.BlockSpec(qblk, lambda i: (i, 0)),
                  pl.BlockSpec(kblk, batch_map),
                  pl.BlockSpec(kblk, batch_map)],
        out_specs=pl.BlockSpec(qblk, lambda i: (i, 0)),
        compiler_params=_params(("parallel",),
                                [(qblk, BF16)] * 2 + [((MEM_LEN, D_MODEL), BF16)] * 2,
                                temps=4 * _nbytes((tq, MEM_LEN), F32) + _nbytes(qblk, F32)),
        name="xattn_prompt",
    )(q, k16, v16)


LANES = 128
HEAD_LANE_TILES = X_HEAD_DIM // LANES
KV_ROWS = MEM_LEN * HEAD_LANE_TILES * X_HEADS
KV_BB = 4


def _cache_rows(c):
    d, b = c.shape[:2]
    c = c.reshape(d, b, MEM_LEN, X_HEADS, HEAD_LANE_TILES, LANES).transpose(0, 1, 2, 4, 3, 5)
    return c.reshape(d, b, KV_ROWS, LANES)


def _head_rows(ref, i, h):
    stride = HEAD_LANE_TILES * X_HEADS
    tiles = [ref[i, pl.ds(t * X_HEADS + h, MEM_LEN, stride=stride), :]
             for t in range(HEAD_LANE_TILES)]
    return jnp.concatenate(tiles, axis=1)


def _xattn_sample_kernel(q_ref, k_ref, v_ref, o_in, o_ref, acc):
    del o_in
    c = pl.program_id(1)
    scale = X_HEAD_DIM ** -0.5

    @pl.when(c == 0)
    def _():
        acc[...] = jnp.zeros_like(acc)

    owner = lax.broadcasted_iota(jnp.int32, (SBLK, MEM_LEN), 0) % SB
    for h in range(X_HEADS):
        hs = slice(h * X_HEAD_DIM, (h + 1) * X_HEAD_DIM)
        keys = jnp.concatenate([_head_rows(k_ref, ii, h) for ii in range(KV_BB)], axis=0)
        s = lax.dot_general(q_ref[:, hs], keys.astype(BF16), _NT,
                            preferred_element_type=F32) * scale
        pr = [jnp.where(owner == c * KV_BB + ii,
                        _softmax_rows(s[:, ii * MEM_LEN:(ii + 1) * MEM_LEN]), 0.0)
              for ii in range(KV_BB)]
        vals = jnp.concatenate([_head_rows(v_ref, ii, h) for ii in range(KV_BB)], axis=0)
        acc[:, hs] += _dot(jnp.concatenate(pr, axis=1).astype(BF16), vals.astype(BF16))

    @pl.when(c == pl.num_programs(1) - 1)
    def _():
        o_ref[...] = acc[...].astype(o_ref.dtype)


def _xattn_sample_call(q, mk, mv, layer, o):
    qblk = (SBLK, D_MODEL)
    kblk = (None, KV_BB, KV_ROWS, LANES)
    blk0 = T_P // SBLK
    per_blk = SB // KV_BB
    return pl.pallas_call(
        _xattn_sample_kernel,
        out_shape=jax.ShapeDtypeStruct((T, D_MODEL), BF16),
        grid=(N_SBLK, per_blk),
        in_specs=[pl.BlockSpec(qblk, lambda j, c: (blk0 + j, 0)),
                  pl.BlockSpec(kblk, lambda j, c: (layer, j * per_blk + c, 0, 0)),
                  pl.BlockSpec(kblk, lambda j, c: (layer, j * per_blk + c, 0, 0)),
                  pl.BlockSpec(memory_space=pl.ANY)],
        out_specs=pl.BlockSpec(qblk, lambda j, c: (blk0 + j, 0)),
        scratch_shapes=[pltpu.VMEM(qblk, F32)],
        input_output_aliases={3: 0},
        compiler_params=_params(("parallel", "arbitrary"),
                                [(qblk, BF16)] * 2 + [((KV_BB, KV_ROWS, LANES), F32)] * 2,
                                resident=_nbytes(qblk, F32),
                                temps=4 * KV_BB * _nbytes((MEM_LEN, X_HEAD_DIM), F32)),
        name="xattn_sample",
    )(q, mk, mv, o)


def _resid_kernel(a_ref, w_ref, x_ref, gn_ref, xo_ref, ho_ref):
    xn = x_ref[...] + _dot(a_ref[...], w_ref[...])
    xo_ref[...] = xn
    ho_ref[...] = _rms(xn, gn_ref[...]).astype(ho_ref.dtype)


def _resid_call(a, w, layer, x, gn, tm=512):
    blk = (tm, D_MODEL)
    row_map = lambda i: (i, 0)
    return pl.pallas_call(
        _resid_kernel,
        out_shape=(jax.ShapeDtypeStruct((T, D_MODEL), F32),
                   jax.ShapeDtypeStruct((T, D_MODEL), BF16)),
        grid=(T // tm,),
        in_specs=[pl.BlockSpec(blk, row_map),
                  _resident((None, D_MODEL, D_MODEL), lambda i: (layer, 0, 0)),
                  pl.BlockSpec(blk, row_map),
                  _layer_spec((1, D_MODEL), layer)],
        out_specs=(pl.BlockSpec(blk, row_map), pl.BlockSpec(blk, row_map)),
        compiler_params=_params(("parallel",),
                                [(blk, BF16)] * 2 + [(blk, F32)] * 2,
                                resident=_nbytes((D_MODEL, D_MODEL), BF16),
                                temps=2 * _nbytes(blk, F32)),
        name="attn_out",
    )(a, w, x, gn)


def _mlp_kernel(n_prompt, h_ref, wu_ref, wd_ref, x_ref, gn_ref, o0_ref, o1_ref, acc):
    i, f = pl.program_id(0), pl.program_id(1)

    @pl.when(f == 0)
    def _():
        acc[...] = x_ref[...]

    hid = jnp.maximum(_dot(h_ref[...], wu_ref[...]), 0.0)
    acc[...] += _dot((hid * hid).astype(BF16), wd_ref[...])

    @pl.when(f == pl.num_programs(1) - 1)
    def _():
        xn = acc[...]
        if n_prompt is None:
            o0_ref[...] = xn
            o1_ref[...] = _rms(xn, gn_ref[...]).astype(o1_ref.dtype)
        else:
            y = _rms(xn, gn_ref[...])

            @pl.when(i < n_prompt)
            def _():
                o0_ref[...] = y

            @pl.when(i >= n_prompt)
            def _():
                o1_ref[...] = y


def _mlp_call(h, wu, wd, layer, x, gn, final, tm=512, tf=1024):
    blk = (tm, D_MODEL)
    row_map = lambda i, f: (i, 0)
    if final:
        n_prompt = T_P // tm
        out_shape = (jax.ShapeDtypeStruct((T_P, D_MODEL), F32),
                     jax.ShapeDtypeStruct((T_S, D_MODEL), F32))
        out_specs = (pl.BlockSpec(blk, lambda i, f: (jnp.minimum(i, n_prompt - 1), 0)),
                     pl.BlockSpec(blk, lambda i, f: (jnp.maximum(i - n_prompt, 0), 0)))
        out_dtype = F32
    else:
        n_prompt = None
        out_shape = (jax.ShapeDtypeStruct((T, D_MODEL), F32),
                     jax.ShapeDtypeStruct((T, D_MODEL), BF16))
        out_specs = (pl.BlockSpec(blk, row_map), pl.BlockSpec(blk, row_map))
        out_dtype = BF16
    return pl.pallas_call(
        functools.partial(_mlp_kernel, n_prompt),
        out_shape=out_shape,
        grid=(T // tm, D_FF // tf),
        in_specs=[pl.BlockSpec(blk, row_map),
                  pl.BlockSpec((None, D_MODEL, tf), lambda i, f: (layer, 0, f)),
                  pl.BlockSpec((None, tf, D_MODEL), lambda i, f: (layer, f, 0)),
                  pl.BlockSpec(blk, row_map),
                  _layer_spec((1, D_MODEL), layer)],
        out_specs=out_specs,
        scratch_shapes=[pltpu.VMEM(blk, F32)],
        compiler_params=_params(
            ("arbitrary", "arbitrary"),
            [(blk, BF16), ((D_MODEL, tf), BF16), ((tf, D_MODEL), BF16), (blk, F32), (blk, F32),
             (blk, out_dtype)],
            resident=_nbytes(blk, F32),
            temps=2 * _nbytes((tm, tf), F32) + _nbytes(blk, F32)),
        name="mlp_final" if final else "mlp",
    )(h, wu, wd, x, gn)


def _sample_to_rows(a):
    c = a.shape[-1]
    return a.reshape(N_SBLK, SB, DEC_SEQ, c).transpose(0, 2, 1, 3).reshape(T_S, c)


def _rows_to_sample(a):
    c = a.shape[-1]
    return a.reshape(N_SBLK, DEC_SEQ, SB, c).transpose(0, 2, 1, 3).reshape(DEC_BATCH, DEC_SEQ, c)


def kernel(x_prompt, x_sample, mem_prompt, cache_mem_k, cache_mem_v, state_conv, state_pool,
           g_mix, w_in, ln_v_g, ln_v_b, w_s, b_s, conv_w, conv_b, ln_c_g, ln_c_b,
           pool_w, pool_scale, w_branch, w_out, g_xattn, g_mem, w_xq, w_xk, w_xv, w_xo,
           g_mlp, w_up, w_down, g_final):
    assert T_S == 512 and T_P % 512 == 0
    rows = lambda a: a.reshape(DEPTH, 1, -1)
    mem = mem_prompt.reshape(BATCH * MEM_LEN, D_MODEL)
    cache_k, cache_v = _cache_rows(cache_mem_k), _cache_rows(cache_mem_v)
    wb16, wo16 = w_branch.astype(BF16), w_out.astype(BF16)
    wk16, wv16, wxo16 = w_xk.astype(BF16), w_xv.astype(BF16), w_xo.astype(BF16)
    wu16, wd16, pw = w_up.astype(BF16), w_down.astype(BF16), pool_w.astype(BF16)
    n1 = D_BR // 1024

    g_mix_r, g_xattn_r, g_mlp_r = rows(g_mix), rows(g_xattn), rows(g_mlp)
    g_next_r = rows(jnp.concatenate([g_mix[1:], g_final[None]], axis=0))
    lvg, lvb = rows(ln_v_g), rows(ln_v_b)
    cb, lg, lb, psc = rows(conv_b), rows(ln_c_g), rows(ln_c_b), rows(pool_scale)
    bs_full = jnp.repeat(b_s.transpose(0, 2, 1), DH_A, axis=2)
    wsm = jnp.repeat(w_s[:, :, :DEC_SEQ, :DEC_SEQ].transpose(0, 2, 3, 1), DH_A, axis=3)
    bsm = jnp.repeat(b_s[:, :, :DEC_SEQ].transpose(0, 2, 1), DH_A, axis=2)
    cpast = state_conv.transpose(0, 2, 1, 3)
    ppast = state_pool.transpose(0, 2, 1, 3)

    x, h = _norm_call(x_prompt.reshape(T_P, D_MODEL), _sample_to_rows(x_sample), g_mix_r)
    k32, v32, k16, v16 = _memkv_call(mem, rows(g_mem), wk16, wv16)
    conv_p, pool_p, glu_s, p_s, chunk_s = [], [], [], [], []
    y_prompt = y_sample = None
    for l in range(DEPTH):
        ug = _proj_call(h, w_in, l, col0=0, ncols=n1, mode="gelu", out_dtype=F32, name="proj_u")
        vn = _proj_call(h, w_in, l, col0=n1, ncols=n1, mode="gelu_ln", out_dtype=F32,
                        name="proj_v", extra=(lvg, lvb))
        glu = _proj_call(h, w_in, l, col0=2 * D_A // 512, ncols=D_CONV // 512, mode="glu",
                         out_dtype=F32, name="proj_glu", tn=512)
        p = _proj_call(h, w_in, l, col0=4 * n1, ncols=n1, mode="id", out_dtype=F32,
                       name="proj_p")
        gates = _proj_call(h, w_in, l, col0=5 * n1, ncols=N_BRANCH * D_MODEL // 1024,
                           mode="sigmoid", out_dtype=BF16, name="proj_gate")

        ya, yb, yc = _mix_prompt_call(ug, vn, glu, p, w_s, bs_full, conv_w, cb, lg, lb, pw, psc, l)
        ya, yb, yc = _mix_sample_call(ug, vn, glu, p, cpast, ppast, wsm, bsm, conv_w, cb, lg, lb,
                                      pw, psc, l, ya, yb, yc)

        x, hx = _merge_call(ya, yb, yc, gates, x, wb16, wo16, l, g_xattn_r)

        q = _proj_call(hx, w_xq, l, col0=0, ncols=D_MODEL // 1024, mode="id", out_dtype=BF16,
                       name="proj_q")
        o = _xattn_prompt_call(q, k16, v16, l)
        o = _xattn_sample_call(q, cache_k, cache_v, l, o)
        x, hm = _resid_call(o, wxo16, l, x, g_mlp_r)

        if l == DEPTH - 1:
            y_prompt, y_sample = _mlp_call(hm, wu16, wd16, l, x, g_next_r, True)
        else:
            x, h = _mlp_call(hm, wu16, wd16, l, x, g_next_r, False)

        conv_p.append(jnp.stack([glu[(b + 1) * SEQ - (CONV_K - 1):(b + 1) * SEQ]
                                 for b in range(BATCH)]))
        pool_p.append(jnp.stack([p[(b + 1) * SEQ - POOL_PAST:(b + 1) * SEQ]
                                 for b in range(BATCH)]))
        glu_s.append(_rows_to_sample(glu[T_P:]))
        p_s.append(_rows_to_sample(p[T_P:]))
        chunk_s.append(_rows_to_sample(vn[T_P:]))

    y_prompt = y_prompt.reshape(BATCH, SEQ, D_MODEL)
    y_sample = _rows_to_sample(y_sample)
    new_conv_s = jnp.concatenate([state_conv[:, :, DEC_SEQ:], jnp.stack(glu_s)], axis=2)
    new_pool_s = jnp.concatenate([state_pool[:, :, DEC_SEQ:], jnp.stack(p_s)], axis=2)
    return (y_prompt, y_sample, _rows_to_cache(k32), _rows_to_cache(v32), jnp.stack(conv_p),
            jnp.stack(pool_p), new_conv_s, new_pool_s, jnp.stack(chunk_s))
```

```python
import functools

import jax
import jax.numpy as jnp
from jax import lax
from jax.experimental import pallas as pl
from jax.experimental.pallas import tpu as pltpu

D_MODEL = 2048
BATCH = 4
SEQ = 2048
DEPTH = 2
DEC_BATCH = 128
DEC_SEQ = 4
PAST_LEN = 16384
D_BR = D_MODEL // 2
D_A = D_BR
H_A = 4
DH_A = D_A // H_A
CHUNK = 128
D_CONV = D_BR
CONV_K = 31
D_POOL = D_BR
POOL_WINDOWS = (2, 4, 8, 16)
N_POOL = len(POOL_WINDOWS)
G_POOL = D_POOL // N_POOL
POOL_PAST = max(POOL_WINDOWS) - 1
MEM_LEN = 256
X_HEADS = 4
X_HEAD_DIM = D_MODEL // X_HEADS
D_FF = 4 * D_MODEL
N_BRANCH = 3
D_IN = 2 * D_A + 2 * D_CONV + D_POOL + N_BRANCH * D_MODEL
RMS_EPS = 1e-6
LN_EPS = 1e-5

T_P = BATCH * SEQ
T_S = DEC_BATCH * DEC_SEQ
T = T_P + T_S
SB = 16
SBLK = DEC_SEQ * SB
N_SBLK = DEC_BATCH // SB

SUBLANES = 8
VMEM_PHYSICAL_BYTES = 64 * 1024 * 1024
VMEM_CEILING_BYTES = VMEM_PHYSICAL_BYTES - 6 * 1024 * 1024

F32 = jnp.float32
BF16 = jnp.bfloat16


def _nbytes(shape, dtype):
    n = 1
    for s in shape:
        n *= s
    return n * jnp.dtype(dtype).itemsize


def _params(semantics, pipelined, resident=0, temps=0):
    need = 2 * sum(_nbytes(s, d) for s, d in pipelined) + resident + temps
    need = need + need // 8 + (2 << 20)
    return pltpu.CompilerParams(
        dimension_semantics=semantics,
        vmem_limit_bytes=int(min(max(need, 16 << 20), VMEM_CEILING_BYTES)))


def _resident(shape, index_map):
    return pl.BlockSpec(shape, index_map, pipeline_mode=pl.Buffered(1))


def _layer_spec(shape, layer):
    zeros = (0,) * len(shape)
    return pl.BlockSpec((None,) + tuple(shape), lambda *_: (layer,) + zeros)


def _sigmoid(x):
    return 0.5 * jnp.tanh(0.5 * x) + 0.5


def _rms(x, g):
    return x * lax.rsqrt(jnp.mean(x * x, axis=-1, keepdims=True) + RMS_EPS) * g


def _ln(x, g, b):
    mu = jnp.mean(x, axis=-1, keepdims=True)
    xc = x - mu
    var = jnp.mean(xc * xc, axis=-1, keepdims=True)
    return xc * lax.rsqrt(var + LN_EPS) * g + b


def _dot(a, b):
    return jnp.dot(a, b, preferred_element_type=F32)


def _norm_kernel(n_prompt, xp_ref, xs_ref, g_ref, h_ref):
    def emit(src):
        h_ref[...] = _rms(src[...], g_ref[...]).astype(h_ref.dtype)

    @pl.when(pl.program_id(0) < n_prompt)
    def _():
        emit(xp_ref)

    @pl.when(pl.program_id(0) >= n_prompt)
    def _():
        emit(xs_ref)


def _norm_call(xp, xs, g, tm=512):
    n_prompt = T_P // tm
    blk = (tm, D_MODEL)
    return pl.pallas_call(
        functools.partial(_norm_kernel, n_prompt),
        out_shape=jax.ShapeDtypeStruct((T, D_MODEL), BF16),
        grid=(T // tm,),
        in_specs=[pl.BlockSpec(blk, lambda i: (jnp.minimum(i, n_prompt - 1), 0)),
                  pl.BlockSpec(blk, lambda i: (jnp.maximum(i - n_prompt, 0), 0)),
                  _layer_spec((1, D_MODEL), 0)],
        out_specs=pl.BlockSpec(blk, lambda i: (i, 0)),
        compiler_params=_params(("arbitrary",), [(blk, F32)] * 2 + [(blk, BF16)],
                                temps=2 * _nbytes(blk, F32)),
        name="rmsnorm_in",
    )(xp, xs, g)


def _proj_kernel(mode, *refs):
    n_w = 2 if mode == "glu" else 1
    n_extra = 2 if mode == "gelu_ln" else 0
    h_ref, w_refs = refs[0], refs[1:1 + n_w]
    extra = refs[1 + n_w:1 + n_w + n_extra]
    o_ref = refs[1 + n_w + n_extra]
    wbf = refs[2 + n_w + n_extra:]

    @pl.when(pl.program_id(1) == 0)
    def _():
        for src, dst in zip(w_refs, wbf):
            dst[...] = src[...].astype(BF16)

    h = h_ref[...]
    z = _dot(h, wbf[0][...])
    if mode == "glu":
        z = z * _sigmoid(_dot(h, wbf[1][...]))
    elif mode == "gelu":
        z = jax.nn.gelu(z)
    elif mode == "gelu_ln":
        z = _ln(jax.nn.gelu(z), extra[0][...], extra[1][...])
    elif mode == "sigmoid":
        z = _sigmoid(z)
    o_ref[...] = z.astype(o_ref.dtype)


def _proj_call(h, w, layer, *, col0, ncols, mode, out_dtype, name, extra=(), tm=1088, tn=1024):
    n, k = h.shape
    hblk, wblk, oblk = (tm, k), (k, tn), (tm, tn)
    n_w = 2 if mode == "glu" else 1
    in_specs = [pl.BlockSpec(hblk, lambda j, i: (i, 0)),
                pl.BlockSpec((None,) + wblk, lambda j, i: (layer, 0, col0 + j))]
    args = [h, w]
    if mode == "glu":
        in_specs.append(pl.BlockSpec((None,) + wblk, lambda j, i: (layer, 0, col0 + ncols + j)))
        args.append(w)
    for e in extra:
        in_specs.append(_layer_spec((1, tn), layer))
        args.append(e)
    return pl.pallas_call(
        functools.partial(_proj_kernel, mode),
        out_shape=jax.ShapeDtypeStruct((n, ncols * tn), out_dtype),
        grid=(ncols, n // tm),
        in_specs=in_specs,
        out_specs=pl.BlockSpec(oblk, lambda j, i: (i, j)),
        scratch_shapes=[pltpu.VMEM(wblk, BF16)] * n_w,
        compiler_params=_params(("arbitrary", "arbitrary"),
                                [(hblk, h.dtype), (oblk, out_dtype)] + [(wblk, F32)] * n_w,
                                resident=n_w * _nbytes(wblk, BF16),
                                temps=(2 + n_w) * _nbytes(oblk, F32)),
        name=name,
    )(*args)


MIX_TM = 256
MIX_RB = 32
CONV_HALO = 32
POOL_HALO = 16


def _conv_ln_silu(window, cw_ref, cb, lg, lb):
    acc = cb + cw_ref[0:1, :] * window(0)
    for k in range(1, CONV_K):
        acc = acc + cw_ref[k:k + 1, :] * window(k)
    y = _ln(acc, lg, lb)
    return y * _sigmoid(y)


def _mix_prompt_kernel(*refs):
    outs = refs[14:17]
    is_prompt = pl.program_id(0) < T_P // MIX_TM

    @pl.when(is_prompt)
    def _():
        _mix_prompt_tile(*refs)

    @pl.when(jnp.logical_not(is_prompt))
    def _():
        for o_ref in outs:
            o_ref[...] = jnp.zeros_like(o_ref)


def _mix_prompt_tile(ug_ref, vn_ref, glu_ref, gh_ref, p_ref, ph_ref,
                     ws_ref, bs_ref, cw_ref, cb_ref, lg_ref, lb_ref, pw_ref, psc_ref,
                     ya_ref, yb_ref, yc_ref, gwin, gshift, pwin, cwb):
    tm = MIX_TM
    seq_tile = pl.program_id(0) % (SEQ // tm)
    first = seq_tile == 0

    row = lax.broadcasted_iota(jnp.int32, (CHUNK, CHUNK), 0)
    col = lax.broadcasted_iota(jnp.int32, (CHUNK, CHUNK), 1)
    for h in range(H_A):
        w = jnp.where(row >= col, ws_ref[h], 0.0).astype(BF16)
        hs = slice(h * DH_A, (h + 1) * DH_A)
        for c in range(tm // CHUNK):
            rs = slice(c * CHUNK, (c + 1) * CHUNK)
            mixed = _dot(w, vn_ref[rs, hs].astype(BF16)) + bs_ref[:, hs]
            ya_ref[rs, hs] = (ug_ref[rs, hs] * mixed).astype(ya_ref.dtype)

    gwin[0:CONV_HALO, :] = jnp.where(first, 0.0, gh_ref[...])
    gwin[CONV_HALO:, :] = glu_ref[...]
    n_shift = CONV_HALO + tm - SUBLANES
    for r in range(1, SUBLANES):
        gshift[r - 1, 0:n_shift, :] = gwin[r:r + n_shift, :]
    cb, lg, lb = cb_ref[...], lg_ref[...], lb_ref[...]
    off = CONV_HALO - (CONV_K - 1)

    def window(r0, k):
        a, r = divmod(off + k, SUBLANES)
        lo = r0 + SUBLANES * a
        if r == 0:
            return gwin[lo:lo + MIX_RB, :]
        return gshift[r - 1, lo:lo + MIX_RB, :]

    for k in range(CONV_K):
        cwb[k] = jnp.broadcast_to(cw_ref[k:k + 1, :], (MIX_RB, D_CONV))
    for rb in range(tm // MIX_RB):
        r0 = rb * MIX_RB
        acc = cb + cwb[0] * window(r0, 0)
        for k in range(1, CONV_K):
            acc = acc + cwb[k] * window(r0, k)
        y = _ln(acc, lg, lb)
        yb_ref[r0:r0 + MIX_RB, :] = (y * _sigmoid(y)).astype(yb_ref.dtype)

    pwin[0:POOL_HALO, :] = jnp.where(first, 0.0, ph_ref[...])
    pwin[POOL_HALO:, :] = p_ref[...]
    pos = seq_tile * tm + lax.broadcasted_iota(jnp.int32, (tm, 1), 0)
    for g, wlen in enumerate(POOL_WINDOWS):
        gs = slice(g * G_POOL, (g + 1) * G_POOL)
        s = pwin[POOL_HALO:POOL_HALO + tm, gs]
        for d in range(1, wlen):
            s = s + pwin[POOL_HALO - d:POOL_HALO - d + tm, gs]
        inv = 1.0 / jnp.minimum(wlen, pos + 1).astype(F32)
        mixed = s * inv - p_ref[:, gs]
        y = _dot(mixed.astype(BF16), pw_ref[g]) * psc_ref[:, gs]
        yc_ref[:, gs] = y.astype(yc_ref.dtype)


def _mix_prompt_call(ug, vn, glu, p, ws, bs_full, cw, cb, lg, lb, pw, psc, layer):
    tm = MIX_TM
    blk = (tm, D_BR)
    last = T_P // tm - 1
    row_map = lambda i: (i, 0)
    in_map = lambda i: (jnp.minimum(i, last), 0)
    in_specs = [
        pl.BlockSpec(blk, in_map),
        pl.BlockSpec(blk, in_map),
        pl.BlockSpec(blk, in_map),
        pl.BlockSpec((CONV_HALO, D_BR), lambda i: (
            jnp.maximum(jnp.minimum(i, last) * (tm // CONV_HALO) - 1, 0), 0)),
        pl.BlockSpec(blk, in_map),
        pl.BlockSpec((POOL_HALO, D_BR), lambda i: (
            jnp.maximum(jnp.minimum(i, last) * (tm // POOL_HALO) - 1, 0), 0)),
        _layer_spec((H_A, CHUNK, CHUNK), layer),
        _layer_spec((CHUNK, D_A), layer),
        _layer_spec((CONV_K, D_CONV), layer),
        _layer_spec((1, D_CONV), layer),
        _layer_spec((1, D_CONV), layer),
        _layer_spec((1, D_CONV), layer),
        _layer_spec((N_POOL, G_POOL, G_POOL), layer),
        _layer_spec((1, D_POOL), layer),
    ]
    out = jax.ShapeDtypeStruct((T, D_BR), BF16)
    return pl.pallas_call(
        _mix_prompt_kernel,
        out_shape=(out, out, out),
        grid=(T // tm,),
        in_specs=in_specs,
        out_specs=(pl.BlockSpec(blk, row_map),) * 3,
        scratch_shapes=[pltpu.VMEM((CONV_HALO + tm, D_CONV), F32),
                        pltpu.VMEM((SUBLANES - 1, CONV_HALO + tm, D_CONV), F32),
                        pltpu.VMEM((POOL_HALO + tm, D_POOL), F32),
                        pltpu.VMEM((CONV_K, MIX_RB, D_CONV), F32)],
        compiler_params=_params(
            ("parallel",),
            [(blk, F32)] * 4 + [(blk, BF16)] * 3 + [((CONV_HALO + POOL_HALO, D_BR), F32),
                                                    ((CHUNK + CONV_K + 8, D_BR), F32)],
            resident=_nbytes((SUBLANES * (CONV_HALO + tm) + POOL_HALO + tm
                              + CONV_K * MIX_RB, D_BR), F32),
            temps=20 * _nbytes(blk, F32)),
        name="mix_prompt",
    )(ug, vn, glu, glu, p, p, ws, bs_full, cw, cb, lg, lb, pw, psc)


def _mix_sample_kernel(ug_ref, vn_ref, glu_ref, p_ref, cpast_ref, ppast_ref,
                       wsm_ref, bsm_ref, cw_ref, cb_ref, lg_ref, lb_ref, pw_ref, psc_ref,
                       ya_in, yb_in, yc_in, ya_ref, yb_ref, yc_ref, mix_scr):
    del ya_in, yb_in, yc_in
    cb, lg, lb = cb_ref[...], lg_ref[...], lb_ref[...]

    def rows(t):
        return slice(t * SB, (t + 1) * SB)

    def conv_src(j):
        return cpast_ref[j] if j < CONV_K - 1 else glu_ref[rows(j - (CONV_K - 1)), :]

    def pool_src(j, gs):
        return ppast_ref[j, :, gs] if j < POOL_PAST else p_ref[rows(j - POOL_PAST), gs]

    for t in range(DEC_SEQ):
        m = bsm_ref[t:t + 1, :] + wsm_ref[t, 0:1, :] * vn_ref[rows(0), :]
        for s in range(1, t + 1):
            m = m + wsm_ref[t, s:s + 1, :] * vn_ref[rows(s), :]
        ya_ref[rows(t), :] = (ug_ref[rows(t), :] * m).astype(ya_ref.dtype)
        y = _conv_ln_silu(lambda k: conv_src(t + k), cw_ref, cb, lg, lb)
        yb_ref[rows(t), :] = y.astype(yb_ref.dtype)
        for g, wlen in enumerate(POOL_WINDOWS):
            gs = slice(g * G_POOL, (g + 1) * G_POOL)
            s = pool_src(POOL_PAST + t, gs)
            for d in range(1, wlen):
                s = s + pool_src(POOL_PAST + t - d, gs)
            cnt = float(min(wlen, PAST_LEN + t + 1))
            mix_scr[rows(t), gs] = s / cnt - p_ref[rows(t), gs]
    for g in range(N_POOL):
        gs = slice(g * G_POOL, (g + 1) * G_POOL)
        y = _dot(mix_scr[:, gs].astype(BF16), pw_ref[g]) * psc_ref[:, gs]
        yc_ref[:, gs] = y.astype(yc_ref.dtype)


def _mix_sample_call(ug, vn, glu, p, cpast, ppast, wsm, bsm, cw, cb, lg, lb, pw, psc, layer,
                     ya, yb, yc):
    blk = (SBLK, D_BR)
    blk0 = T_P // SBLK
    row_map = lambda j: (blk0 + j, 0)
    any_spec = pl.BlockSpec(memory_space=pl.ANY)
    in_specs = [
        pl.BlockSpec(blk, row_map), pl.BlockSpec(blk, row_map),
        pl.BlockSpec(blk, row_map), pl.BlockSpec(blk, row_map),
        pl.BlockSpec((None, CONV_K - 1, SB, D_CONV), lambda j: (layer, 0, j, 0)),
        pl.BlockSpec((None, POOL_PAST, SB, D_POOL), lambda j: (layer, 0, j, 0)),
        _layer_spec((DEC_SEQ, DEC_SEQ, D_A), layer),
        _layer_spec((DEC_SEQ, D_A), layer),
        _layer_spec((CONV_K, D_CONV), layer),
        _layer_spec((1, D_CONV), layer),
        _layer_spec((1, D_CONV), layer),
        _layer_spec((1, D_CONV), layer),
        _layer_spec((N_POOL, G_POOL, G_POOL), layer),
        _layer_spec((1, D_POOL), layer),
        any_spec, any_spec, any_spec,
    ]
    out = jax.ShapeDtypeStruct((T, D_BR), BF16)
    return pl.pallas_call(
        _mix_sample_kernel,
        out_shape=(out, out, out),
        grid=(N_SBLK,),
        in_specs=in_specs,
        out_specs=(pl.BlockSpec(blk, row_map),) * 3,
        scratch_shapes=[pltpu.VMEM(blk, F32)],
        input_output_aliases={14: 0, 15: 1, 16: 2},
        compiler_params=_params(
            ("parallel",),
            [(blk, F32)] * 4 + [(blk, BF16)] * 3
            + [((CONV_K - 1 + POOL_PAST, SB, D_BR), F32), ((CONV_K + 16, D_BR), F32)],
            resident=_nbytes(blk, F32), temps=32 * _nbytes(blk, F32)),
        name="mix_sample",
    )(ug, vn, glu, p, cpast, ppast, wsm, bsm, cw, cb, lg, lb, pw, psc, ya, yb, yc)


def _merge_kernel(n_prompt, ya_ref, yb_ref, yc_ref, g0_ref, g1_ref, g2_ref, *refs):
    if n_prompt is None:
        x_ref, wb_ref, wo_ref, gn_ref, xo_ref, ho_ref = refs
        x = x_ref[...]
    else:
        xp_ref, xs_ref, wb_ref, wo_ref, gn_ref, xo_ref, ho_ref = refs
        x = jnp.where(pl.program_id(0) < n_prompt, xp_ref[...], xs_ref[...])
    merged = g0_ref[...] * _dot(ya_ref[...], wb_ref[0])
    merged = merged + g1_ref[...] * _dot(yb_ref[...], wb_ref[1])
    merged = merged + g2_ref[...] * _dot(yc_ref[...], wb_ref[2])
    xn = x + _dot(merged.astype(BF16), wo_ref[...])
    xo_ref[...] = xn
    ho_ref[...] = _rms(xn, gn_ref[...]).astype(ho_ref.dtype)


def _merge_call(ya, yb, yc, gates, x, wb, wo, layer, gn, tm=256):
    yblk, xblk = (tm, D_BR), (tm, D_MODEL)
    row_map = lambda i: (i, 0)
    if isinstance(x, tuple):
        n_prompt = T_P // tm
        x_specs = [pl.BlockSpec(xblk, lambda i: (jnp.minimum(i, n_prompt - 1), 0)),
                   pl.BlockSpec(xblk, lambda i: (jnp.maximum(i - n_prompt, 0), 0))]
    else:
        n_prompt, x, x_specs = None, (x,), [pl.BlockSpec(xblk, row_map)]
    in_specs = [
        pl.BlockSpec(yblk, row_map), pl.BlockSpec(yblk, row_map), pl.BlockSpec(yblk, row_map),
        pl.BlockSpec(xblk, lambda i: (i, 0)),
        pl.BlockSpec(xblk, lambda i: (i, 1)),
        pl.BlockSpec(xblk, lambda i: (i, 2)),
        *x_specs,
        _resident((None, N_BRANCH, D_BR, D_MODEL), lambda i: (layer, 0, 0, 0)),
        _resident((None, D_MODEL, D_MODEL), lambda i: (layer, 0, 0)),
        _layer_spec((1, D_MODEL), layer),
    ]
    return pl.pallas_call(
        functools.partial(_merge_kernel, n_prompt),
        out_shape=(jax.ShapeDtypeStruct((T, D_MODEL), F32),
                   jax.ShapeDtypeStruct((T, D_MODEL), BF16)),
        grid=(T // tm,),
        in_specs=in_specs,
        out_specs=(pl.BlockSpec(xblk, row_map), pl.BlockSpec(xblk, row_map)),
        compiler_params=_params(
            ("parallel",),
            [(yblk, BF16)] * 3 + [(xblk, gates.dtype)] * 3 + [(xblk, F32)] * (1 + len(x))
            + [(xblk, BF16)],
            resident=_nbytes((N_BRANCH, D_BR, D_MODEL), BF16) + _nbytes((D_MODEL, D_MODEL), BF16),
            temps=3 * _nbytes(xblk, F32)),
        name="merge_out",
    )(ya, yb, yc, gates, gates, gates, *x, wb, wo, gn)


def _memkv_kernel(m_ref, g_ref, wk_ref, wv_ref, k32_ref, v32_ref, k16_ref, v16_ref):
    m = _rms(m_ref[...], g_ref[...]).astype(BF16)
    stride = HEAD_LANE_TILES * X_HEADS
    for w_ref, o32_ref, o16_ref in ((wk_ref, k32_ref, k16_ref), (wv_ref, v32_ref, v16_ref)):
        z = _dot(m, w_ref[...])
        o16_ref[...] = z.astype(BF16)
        for h in range(X_HEADS):
            for t in range(HEAD_LANE_TILES):
                c0 = h * X_HEAD_DIM + t * LANES
                o32_ref[pl.ds(t * X_HEADS + h, MEM_LEN, stride=stride), :] = z[:, c0:c0 + LANES]


def _memkv_call(mem, g, wk, wv):
    mblk, wblk = (MEM_LEN, D_MODEL), (None, D_MODEL, D_MODEL)
    o32 = jax.ShapeDtypeStruct((DEPTH, BATCH, KV_ROWS, LANES), F32)
    o16 = jax.ShapeDtypeStruct((DEPTH, BATCH * MEM_LEN, D_MODEL), BF16)
    o32_spec = pl.BlockSpec((None, None, KV_ROWS, LANES), lambda l, b: (l, b, 0, 0))
    o16_spec = pl.BlockSpec((None,) + mblk, lambda l, b: (l, b, 0))
    return pl.pallas_call(
        _memkv_kernel,
        out_shape=(o32, o32, o16, o16),
        grid=(DEPTH, BATCH),
        in_specs=[pl.BlockSpec(mblk, lambda l, b: (b, 0)),
                  pl.BlockSpec((None, 1, D_MODEL), lambda l, b: (l, 0, 0)),
                  pl.BlockSpec(wblk, lambda l, b: (l, 0, 0)),
                  pl.BlockSpec(wblk, lambda l, b: (l, 0, 0))],
        out_specs=(o32_spec, o32_spec, o16_spec, o16_spec),
        compiler_params=_params(
            ("parallel", "parallel"),
            [((D_MODEL, D_MODEL), BF16)] * 2 + [(mblk, F32)] * 3 + [(mblk, BF16)] * 2,
            temps=4 * _nbytes(mblk, F32)),
        name="mem_kv",
    )(mem, g, wk, wv)


def _rows_to_cache(c):
    d, b = c.shape[:2]
    c = c.reshape(d, b, MEM_LEN, HEAD_LANE_TILES, X_HEADS, LANES).transpose(0, 1, 2, 4, 3, 5)
    return c.reshape(d, b, MEM_LEN, X_HEADS, X_HEAD_DIM)


_NT = (((1,), (1,)), ((), ()))


def _softmax_rows(s):
    e = jnp.exp(s - jnp.max(s, axis=-1, keepdims=True))
    return e * (1.0 / jnp.sum(e, axis=-1, keepdims=True))


def _xattn_prompt_kernel(n_prompt, q_ref, k_ref, v_ref, o_ref):
    scale = X_HEAD_DIM ** -0.5
    is_prompt = pl.program_id(0) < n_prompt

    @pl.when(is_prompt)
    def _():
        for h in range(X_HEADS):
            hs = slice(h * X_HEAD_DIM, (h + 1) * X_HEAD_DIM)
            s = lax.dot_general(q_ref[:, hs], k_ref[:, hs], _NT,
                                preferred_element_type=F32) * scale
            pr = _softmax_rows(s).astype(BF16)
            o_ref[:, hs] = _dot(pr, v_ref[:, hs]).astype(o_ref.dtype)

    @pl.when(jnp.logical_not(is_prompt))
    def _():
        o_ref[...] = jnp.zeros_like(o_ref)


def _xattn_prompt_call(q, k16, v16, layer, tq=512):
    qblk, kblk = (tq, D_MODEL), (None, MEM_LEN, D_MODEL)
    per_seq = SEQ // tq
    n_prompt = T_P // tq
    batch_map = lambda i: (layer, jnp.minimum(i // per_seq, BATCH - 1), 0)
    return pl.pallas_call(
        functools.partial(_xattn_prompt_kernel, n_prompt),
        out_shape=jax.ShapeDtypeStruct((T, D_MODEL), BF16),
        grid=(T // tq,),
        in_specs=[pl.BlockSpec(qblk, lambda i: (i, 0)),
                  pl.BlockSpec(kblk, batch_map),
                  pl.BlockSpec(kblk, batch_map)],
        out_specs=pl.BlockSpec(qblk, lambda i: (i, 0)),
        compiler_params=_params(("parallel",),
                                [(qblk, BF16)] * 2 + [((MEM_LEN, D_MODEL), BF16)] * 2,
                                temps=4 * _nbytes((tq, MEM_LEN), F32) + _nbytes(qblk, F32)),
        name="xattn_prompt",
    )(q, k16, v16)


LANES = 128
HEAD_LANE_TILES = X_HEAD_DIM // LANES
KV_ROWS = MEM_LEN * HEAD_LANE_TILES * X_HEADS
KV_BB = 4


def _cache_rows(c):
    d, b = c.shape[:2]
    c = c.reshape(d, b, MEM_LEN, X_HEADS, HEAD_LANE_TILES, LANES).transpose(0, 1, 2, 4, 3, 5)
    return c.reshape(d, b, KV_ROWS, LANES)


def _head_rows(ref, i, h):
    stride = HEAD_LANE_TILES * X_HEADS
    tiles = [ref[i, pl.ds(t * X_HEADS + h, MEM_LEN, stride=stride), :]
             for t in range(HEAD_LANE_TILES)]
    return jnp.concatenate(tiles, axis=1)


def _xattn_sample_kernel(q_ref, k_ref, v_ref, o_in, o_ref, acc):
    del o_in
    c = pl.program_id(1)
    scale = X_HEAD_DIM ** -0.5

    @pl.when(c == 0)
    def _():
        acc[...] = jnp.zeros_like(acc)

    owner = lax.broadcasted_iota(jnp.int32, (SBLK, MEM_LEN), 0) % SB
    for h in range(X_HEADS):
        hs = slice(h * X_HEAD_DIM, (h + 1) * X_HEAD_DIM)
        keys = jnp.concatenate([_head_rows(k_ref, ii, h) for ii in range(KV_BB)], axis=0)
        s = lax.dot_general(q_ref[:, hs], keys.astype(BF16), _NT,
                            preferred_element_type=F32) * scale
        pr = [jnp.where(owner == c * KV_BB + ii,
                        _softmax_rows(s[:, ii * MEM_LEN:(ii + 1) * MEM_LEN]), 0.0)
              for ii in range(KV_BB)]
        vals = jnp.concatenate([_head_rows(v_ref, ii, h) for ii in range(KV_BB)], axis=0)
        acc[:, hs] += _dot(jnp.concatenate(pr, axis=1).astype(BF16), vals.astype(BF16))

    @pl.when(c == pl.num_programs(1) - 1)
    def _():
        o_ref[...] = acc[...].astype(o_ref.dtype)


def _xattn_sample_call(q, mk, mv, layer, o):
    qblk = (SBLK, D_MODEL)
    kblk = (None, KV_BB, KV_ROWS, LANES)
    blk0 = T_P // SBLK
    per_blk = SB // KV_BB
    return pl.pallas_call(
        _xattn_sample_kernel,
        out_shape=jax.ShapeDtypeStruct((T, D_MODEL), BF16),
        grid=(N_SBLK, per_blk),
        in_specs=[pl.BlockSpec(qblk, lambda j, c: (blk0 + j, 0)),
                  pl.BlockSpec(kblk, lambda j, c: (layer, j * per_blk + c, 0, 0)),
                  pl.BlockSpec(kblk, lambda j, c: (layer, j * per_blk + c, 0, 0)),
                  pl.BlockSpec(memory_space=pl.ANY)],
        out_specs=pl.BlockSpec(qblk, lambda j, c: (blk0 + j, 0)),
        scratch_shapes=[pltpu.VMEM(qblk, F32)],
        input_output_aliases={3: 0},
        compiler_params=_params(("parallel", "arbitrary"),
                                [(qblk, BF16)] * 2 + [((KV_BB, KV_ROWS, LANES), F32)] * 2,
                                resident=_nbytes(qblk, F32),
                                temps=4 * KV_BB * _nbytes((MEM_LEN, X_HEAD_DIM), F32)),
        name="xattn_sample",
    )(q, mk, mv, o)


def _resid_kernel(a_ref, w_ref, x_ref, gn_ref, xo_ref, ho_ref):
    xn = x_ref[...] + _dot(a_ref[...], w_ref[...])
    xo_ref[...] = xn
    ho_ref[...] = _rms(xn, gn_ref[...]).astype(ho_ref.dtype)


def _resid_call(a, w, layer, x, gn, tm=512):
    blk = (tm, D_MODEL)
    row_map = lambda i: (i, 0)
    return pl.pallas_call(
        _resid_kernel,
        out_shape=(jax.ShapeDtypeStruct((T, D_MODEL), F32),
                   jax.ShapeDtypeStruct((T, D_MODEL), BF16)),
        grid=(T // tm,),
        in_specs=[pl.BlockSpec(blk, row_map),
                  _resident((None, D_MODEL, D_MODEL), lambda i: (layer, 0, 0)),
                  pl.BlockSpec(blk, row_map),
                  _layer_spec((1, D_MODEL), layer)],
        out_specs=(pl.BlockSpec(blk, row_map), pl.BlockSpec(blk, row_map)),
        compiler_params=_params(("parallel",),
                                [(blk, BF16)] * 2 + [(blk, F32)] * 2,
                                resident=_nbytes((D_MODEL, D_MODEL), BF16),
                                temps=2 * _nbytes(blk, F32)),
        name="attn_out",
    )(a, w, x, gn)


def _mlp_kernel(n_prompt, h_ref, wu_ref, wd_ref, x_ref, gn_ref, o0_ref, o1_ref, acc):
    i, f = pl.program_id(0), pl.program_id(1)

    @pl.when(f == 0)
    def _():
        acc[...] = x_ref[...]

    hid = jnp.maximum(_dot(h_ref[...], wu_ref[...]), 0.0)
    acc[...] += _dot((hid * hid).astype(BF16), wd_ref[...])

    @pl.when(f == pl.num_programs(1) - 1)
    def _():
        xn = acc[...]
        if n_prompt is None:
            o0_ref[...] = xn
            o1_ref[...] = _rms(xn, gn_ref[...]).astype(o1_ref.dtype)
        else:
            y = _rms(xn, gn_ref[...])

            @pl.when(i < n_prompt)
            def _():
                o0_ref[...] = y

            @pl.when(i >= n_prompt)
            def _():
                o1_ref[...] = y


def _mlp_call(h, wu, wd, layer, x, gn, final, tm=512, tf=1024):
    blk = (tm, D_MODEL)
    row_map = lambda i, f: (i, 0)
    if final:
        n_prompt = T_P // tm
        out_shape = (jax.ShapeDtypeStruct((T_P, D_MODEL), F32),
                     jax.ShapeDtypeStruct((T_S, D_MODEL), F32))
        out_specs = (pl.BlockSpec(blk, lambda i, f: (jnp.minimum(i, n_prompt - 1), 0)),
                     pl.BlockSpec(blk, lambda i, f: (jnp.maximum(i - n_prompt, 0), 0)))
        out_dtype = F32
    else:
        n_prompt = None
        out_shape = (jax.ShapeDtypeStruct((T, D_MODEL), F32),
                     jax.ShapeDtypeStruct((T, D_MODEL), BF16))
        out_specs = (pl.BlockSpec(blk, row_map), pl.BlockSpec(blk, row_map))
        out_dtype = BF16
    return pl.pallas_call(
        functools.partial(_mlp_kernel, n_prompt),
        out_shape=out_shape,
        grid=(T // tm, D_FF // tf),
        in_specs=[pl.BlockSpec(blk, row_map),
                  pl.BlockSpec((None, D_MODEL, tf), lambda i, f: (layer, 0, f)),
                  pl.BlockSpec((None, tf, D_MODEL), lambda i, f: (layer, f, 0)),
                  pl.BlockSpec(blk, row_map),
                  _layer_spec((1, D_MODEL), layer)],
        out_specs=out_specs,
        scratch_shapes=[pltpu.VMEM(blk, F32)],
        compiler_params=_params(
            ("arbitrary", "arbitrary"),
            [(blk, BF16), ((D_MODEL, tf), BF16), ((tf, D_MODEL), BF16), (blk, F32), (blk, F32),
             (blk, out_dtype)],
            resident=_nbytes(blk, F32),
            temps=2 * _nbytes((tm, tf), F32) + _nbytes(blk, F32)),
        name="mlp_final" if final else "mlp",
    )(h, wu, wd, x, gn)


def _sample_to_rows(a):
    c = a.shape[-1]
    return a.reshape(N_SBLK, SB, DEC_SEQ, c).transpose(0, 2, 1, 3).reshape(T_S, c)


def _rows_to_sample(a):
    c = a.shape[-1]
    return a.reshape(N_SBLK, DEC_SEQ, SB, c).transpose(0, 2, 1, 3).reshape(DEC_BATCH, DEC_SEQ, c)


def kernel(x_prompt, x_sample, mem_prompt, cache_mem_k, cache_mem_v, state_conv, state_pool,
           g_mix, w_in, ln_v_g, ln_v_b, w_s, b_s, conv_w, conv_b, ln_c_g, ln_c_b,
           pool_w, pool_scale, w_branch, w_out, g_xattn, g_mem, w_xq, w_xk, w_xv, w_xo,
           g_mlp, w_up, w_down, g_final):
    assert T_S == 512 and T_P % 512 == 0
    rows = lambda a: a.reshape(DEPTH, 1, -1)
    mem = mem_prompt.reshape(BATCH * MEM_LEN, D_MODEL)
    cache_k, cache_v = _cache_rows(cache_mem_k), _cache_rows(cache_mem_v)
    wb16, wo16 = w_branch.astype(BF16), w_out.astype(BF16)
    wk16, wv16, wxo16 = w_xk.astype(BF16), w_xv.astype(BF16), w_xo.astype(BF16)
    wu16, wd16, pw = w_up.astype(BF16), w_down.astype(BF16), pool_w.astype(BF16)
    n1 = D_BR // 1024

    g_mix_r, g_xattn_r, g_mlp_r = rows(g_mix), rows(g_xattn), rows(g_mlp)
    g_next_r = rows(jnp.concatenate([g_mix[1:], g_final[None]], axis=0))
    lvg, lvb = rows(ln_v_g), rows(ln_v_b)
    cb, lg, lb, psc = rows(conv_b), rows(ln_c_g), rows(ln_c_b), rows(pool_scale)
    bs_full = jnp.repeat(b_s.transpose(0, 2, 1), DH_A, axis=2)
    wsm = jnp.repeat(w_s[:, :, :DEC_SEQ, :DEC_SEQ].transpose(0, 2, 3, 1), DH_A, axis=3)
    bsm = jnp.repeat(b_s[:, :, :DEC_SEQ].transpose(0, 2, 1), DH_A, axis=2)
    cpast = state_conv.transpose(0, 2, 1, 3)
    ppast = state_pool.transpose(0, 2, 1, 3)

    x = (x_prompt.reshape(T_P, D_MODEL), _sample_to_rows(x_sample))
    h = _norm_call(*x, g_mix_r)
    k32, v32, k16, v16 = _memkv_call(mem, rows(g_mem), wk16, wv16)
    conv_p, pool_p, glu_s, p_s, chunk_s = [], [], [], [], []
    y_prompt = y_sample = None
    for l in range(DEPTH):
        ug = _proj_call(h, w_in, l, col0=0, ncols=n1, mode="gelu", out_dtype=F32, name="proj_u")
        vn = _proj_call(h, w_in, l, col0=n1, ncols=n1, mode="gelu_ln", out_dtype=F32,
                        name="proj_v", extra=(lvg, lvb))
        glu = _proj_call(h, w_in, l, col0=2 * D_A // 512, ncols=D_CONV // 512, mode="glu",
                         out_dtype=F32, name="proj_glu", tn=512)
        p = _proj_call(h, w_in, l, col0=4 * n1, ncols=n1, mode="id", out_dtype=F32,
                       name="proj_p")
        gates = _proj_call(h, w_in, l, col0=5 * n1, ncols=N_BRANCH * D_MODEL // 1024,
                           mode="sigmoid", out_dtype=BF16, name="proj_gate")

        ya, yb, yc = _mix_prompt_call(ug, vn, glu, p, w_s, bs_full, conv_w, cb, lg, lb, pw, psc, l)
        ya, yb, yc = _mix_sample_call(ug, vn, glu, p, cpast, ppast, wsm, bsm, conv_w, cb, lg, lb,
                                      pw, psc, l, ya, yb, yc)

        x, hx = _merge_call(ya, yb, yc, gates, x, wb16, wo16, l, g_xattn_r)

        q = _proj_call(hx, w_xq, l, col0=0, ncols=D_MODEL // 1024, mode="id", out_dtype=BF16,
                       name="proj_q")
        o = _xattn_prompt_call(q, k16, v16, l)
        o = _xattn_sample_call(q, cache_k, cache_v, l, o)
        x, hm = _resid_call(o, wxo16, l, x, g_mlp_r)

        if l == DEPTH - 1:
            y_prompt, y_sample = _mlp_call(hm, wu16, wd16, l, x, g_next_r, True)
        else:
            x, h = _mlp_call(hm, wu16, wd16, l, x, g_next_r, False)

        conv_p.append(jnp.stack([glu[(b + 1) * SEQ - (CONV_K - 1):(b + 1) * SEQ]
                                 for b in range(BATCH)]))
        pool_p.append(jnp.stack([p[(b + 1) * SEQ - POOL_PAST:(b + 1) * SEQ]
                                 for b in range(BATCH)]))
        glu_s.append(_rows_to_sample(glu[T_P:]))
        p_s.append(_rows_to_sample(p[T_P:]))
        chunk_s.append(_rows_to_sample(vn[T_P:]))

    y_prompt = y_prompt.reshape(BATCH, SEQ, D_MODEL)
    y_sample = _rows_to_sample(y_sample)
    new_conv_s = jnp.concatenate([state_conv[:, :, DEC_SEQ:], jnp.stack(glu_s)], axis=2)
    new_pool_s = jnp.concatenate([state_pool[:, :, DEC_SEQ:], jnp.stack(p_s)], axis=2)
    return (y_prompt, y_sample, _rows_to_cache(k32), _rows_to_cache(v32), jnp.stack(conv_p),
            jnp.stack(pool_p), new_conv_s, new_pool_s, jnp.stack(chunk_s))
```

```python
import functools

import jax
import jax.numpy as jnp
from jax import lax
from jax.experimental import pallas as pl
from jax.experimental.pallas import tpu as pltpu

D_MODEL = 2048
BATCH = 4
SEQ = 2048
DEPTH = 2
DEC_BATCH = 128
DEC_SEQ = 4
PAST_LEN = 16384
D_BR = D_MODEL // 2
D_A = D_BR
H_A = 4
DH_A = D_A // H_A
CHUNK = 128
D_CONV = D_BR
CONV_K = 31
D_POOL = D_BR
POOL_WINDOWS = (2, 4, 8, 16)
N_POOL = len(POOL_WINDOWS)
G_POOL = D_POOL // N_POOL
POOL_PAST = max(POOL_WINDOWS) - 1
MEM_LEN = 256
X_HEADS = 4
X_HEAD_DIM = D_MODEL // X_HEADS
D_FF = 4 * D_MODEL
N_BRANCH = 3
D_IN = 2 * D_A + 2 * D_CONV + D_POOL + N_BRANCH * D_MODEL
RMS_EPS = 1e-6
LN_EPS = 1e-5

T_P = BATCH * SEQ
T_S = DEC_BATCH * DEC_SEQ
T = T_P + T_S
SB = 16
SBLK = DEC_SEQ * SB
N_SBLK = DEC_BATCH // SB

SUBLANES = 8
VMEM_PHYSICAL_BYTES = 64 * 1024 * 1024
VMEM_CEILING_BYTES = VMEM_PHYSICAL_BYTES - 6 * 1024 * 1024

F32 = jnp.float32
BF16 = jnp.bfloat16


def _nbytes(shape, dtype):
    n = 1
    for s in shape:
        n *= s
    return n * jnp.dtype(dtype).itemsize


def _params(semantics, pipelined, resident=0, temps=0):
    need = 2 * sum(_nbytes(s, d) for s, d in pipelined) + resident + temps
    need = need + need // 8 + (2 << 20)
    return pltpu.CompilerParams(
        dimension_semantics=semantics,
        vmem_limit_bytes=int(min(max(need, 16 << 20), VMEM_CEILING_BYTES)))


def _resident(shape, index_map):
    return pl.BlockSpec(shape, index_map, pipeline_mode=pl.Buffered(1))


def _layer_spec(shape, layer):
    zeros = (0,) * len(shape)
    return pl.BlockSpec((None,) + tuple(shape), lambda *_: (layer,) + zeros)


def _sigmoid(x):
    return 0.5 * jnp.tanh(0.5 * x) + 0.5


def _rms(x, g):
    return x * lax.rsqrt(jnp.mean(x * x, axis=-1, keepdims=True) + RMS_EPS) * g


def _ln(x, g, b):
    mu = jnp.mean(x, axis=-1, keepdims=True)
    xc = x - mu
    var = jnp.mean(xc * xc, axis=-1, keepdims=True)
    return xc * lax.rsqrt(var + LN_EPS) * g + b


def _dot(a, b):
    return jnp.dot(a, b, preferred_element_type=F32)


def _norm_kernel(n_prompt, xp_ref, xs_ref, g_ref, h_ref):
    def emit(src):
        h_ref[...] = _rms(src[...], g_ref[...]).astype(h_ref.dtype)

    @pl.when(pl.program_id(0) < n_prompt)
    def _():
        emit(xp_ref)

    @pl.when(pl.program_id(0) >= n_prompt)
    def _():
        emit(xs_ref)


def _norm_call(xp, xs, g, tm=512):
    n_prompt = T_P // tm
    blk = (tm, D_MODEL)
    return pl.pallas_call(
        functools.partial(_norm_kernel, n_prompt),
        out_shape=jax.ShapeDtypeStruct((T, D_MODEL), BF16),
        grid=(T // tm,),
        in_specs=[pl.BlockSpec(blk, lambda i: (jnp.minimum(i, n_prompt - 1), 0)),
                  pl.BlockSpec(blk, lambda i: (jnp.maximum(i - n_prompt, 0), 0)),
                  _layer_spec((1, D_MODEL), 0)],
        out_specs=pl.BlockSpec(blk, lambda i: (i, 0)),
        compiler_params=_params(("arbitrary",), [(blk, F32)] * 2 + [(blk, BF16)],
                                temps=2 * _nbytes(blk, F32)),
        name="rmsnorm_in",
    )(xp, xs, g)


def _proj_kernel(mode, *refs):
    n_w = 2 if mode == "glu" else 1
    n_extra = 2 if mode == "gelu_ln" else 0
    h_ref, w_refs = refs[0], refs[1:1 + n_w]
    extra = refs[1 + n_w:1 + n_w + n_extra]
    o_ref = refs[1 + n_w + n_extra]
    wbf = refs[2 + n_w + n_extra:]

    @pl.when(pl.program_id(1) == 0)
    def _():
        for src, dst in zip(w_refs, wbf):
            dst[...] = src[...].astype(BF16)

    h = h_ref[...]
    z = _dot(h, wbf[0][...])
    if mode == "glu":
        z = z * _sigmoid(_dot(h, wbf[1][...]))
    elif mode == "gelu":
        z = jax.nn.gelu(z)
    elif mode == "gelu_ln":
        z = _ln(jax.nn.gelu(z), extra[0][...], extra[1][...])
    elif mode == "sigmoid":
        z = _sigmoid(z)
    o_ref[...] = z.astype(o_ref.dtype)


def _proj_call(h, w, layer, *, col0, ncols, mode, out_dtype, name, extra=(), tm=1088, tn=1024):
    n, k = h.shape
    hblk, wblk, oblk = (tm, k), (k, tn), (tm, tn)
    n_w = 2 if mode == "glu" else 1
    in_specs = [pl.BlockSpec(hblk, lambda j, i: (i, 0)),
                pl.BlockSpec((None,) + wblk, lambda j, i: (layer, 0, col0 + j))]
    args = [h, w]
    if mode == "glu":
        in_specs.append(pl.BlockSpec((None,) + wblk, lambda j, i: (layer, 0, col0 + ncols + j)))
        args.append(w)
    for e in extra:
        in_specs.append(_layer_spec((1, tn), layer))
        args.append(e)
    return pl.pallas_call(
        functools.partial(_proj_kernel, mode),
        out_shape=jax.ShapeDtypeStruct((n, ncols * tn), out_dtype),
        grid=(ncols, n // tm),
        in_specs=in_specs,
        out_specs=pl.BlockSpec(oblk, lambda j, i: (i, j)),
        scratch_shapes=[pltpu.VMEM(wblk, BF16)] * n_w,
        compiler_params=_params(("arbitrary", "arbitrary"),
                                [(hblk, h.dtype), (oblk, out_dtype)] + [(wblk, F32)] * n_w,
                                resident=n_w * _nbytes(wblk, BF16),
                                temps=(2 + n_w) * _nbytes(oblk, F32)),
        name=name,
    )(*args)


MIX_TM = 256
MIX_RB = 16
CONV_HALO = 32
POOL_HALO = 16


def _conv_ln_silu(window, cw_ref, cb, lg, lb):
    acc = cb + cw_ref[0:1, :] * window(0)
    for k in range(1, CONV_K):
        acc = acc + cw_ref[k:k + 1, :] * window(k)
    y = _ln(acc, lg, lb)
    return y * _sigmoid(y)


def _mix_prompt_kernel(*refs):
    outs = refs[14:17]
    is_prompt = pl.program_id(0) < T_P // MIX_TM

    @pl.when(is_prompt)
    def _():
        _mix_prompt_tile(*refs)

    @pl.when(jnp.logical_not(is_prompt))
    def _():
        for o_ref in outs:
            o_ref[...] = jnp.zeros_like(o_ref)


def _mix_prompt_tile(ug_ref, vn_ref, glu_ref, gh_ref, p_ref, ph_ref,
                     ws_ref, bs_ref, cw_ref, cb_ref, lg_ref, lb_ref, pw_ref, psc_ref,
                     ya_ref, yb_ref, yc_ref, gwin, gshift, pwin, cwb, plev):
    tm = MIX_TM
    seq_tile = pl.program_id(0) % (SEQ // tm)
    first = seq_tile == 0

    row = lax.broadcasted_iota(jnp.int32, (CHUNK, CHUNK), 0)
    col = lax.broadcasted_iota(jnp.int32, (CHUNK, CHUNK), 1)
    for h in range(H_A):
        w = jnp.where(row >= col, ws_ref[h], 0.0).astype(BF16)
        hs = slice(h * DH_A, (h + 1) * DH_A)
        for c in range(tm // CHUNK):
            rs = slice(c * CHUNK, (c + 1) * CHUNK)
            mixed = _dot(w, vn_ref[rs, hs].astype(BF16)) + bs_ref[:, hs]
            ya_ref[rs, hs] = (ug_ref[rs, hs] * mixed).astype(ya_ref.dtype)

    gwin[0:CONV_HALO, :] = jnp.where(first, 0.0, gh_ref[...])
    gwin[CONV_HALO:, :] = glu_ref[...]
    n_shift = CONV_HALO + tm - SUBLANES
    for r in range(1, SUBLANES):
        gshift[r - 1, 0:n_shift, :] = gwin[r:r + n_shift, :]
    cb, lg, lb = cb_ref[...], lg_ref[...], lb_ref[...]
    off = CONV_HALO - (CONV_K - 1)

    def window(r0, k):
        a, r = divmod(off + k, SUBLANES)
        lo = r0 + SUBLANES * a
        if r == 0:
            return gwin[lo:lo + MIX_RB, :]
        return gshift[r - 1, lo:lo + MIX_RB, :]

    for k in range(CONV_K):
        cwb[k] = jnp.broadcast_to(cw_ref[k:k + 1, :], (MIX_RB, D_CONV))
    for rb in range(tm // MIX_RB):
        r0 = rb * MIX_RB
        acc = cb + cwb[0] * window(r0, 0)
        for k in range(1, CONV_K):
            acc = acc + cwb[k] * window(r0, k)
        y = _ln(acc, lg, lb)
        yb_ref[r0:r0 + MIX_RB, :] = (y * _sigmoid(y)).astype(yb_ref.dtype)

    pwin[0:POOL_HALO, :] = jnp.where(first, 0.0, ph_ref[...])
    pwin[POOL_HALO:, :] = p_ref[...]
    pos = seq_tile * tm + lax.broadcasted_iota(jnp.int32, (tm, 1), 0)
    for g, wlen in enumerate(POOL_WINDOWS):
        gs = slice(g * G_POOL, (g + 1) * G_POOL)
        src, span, n_rows = pwin, 1, POOL_HALO + tm
        while span < wlen:
            n_rows -= span
            plev[0:n_rows, gs] = src[0:n_rows, gs] + src[span:span + n_rows, gs]
            src, span = plev, 2 * span
        first_row = POOL_HALO + 1 - wlen
        s = src[first_row:first_row + tm, gs]
        inv = 1.0 / jnp.minimum(wlen, pos + 1).astype(F32)
        mixed = s * inv - p_ref[:, gs]
        y = _dot(mixed.astype(BF16), pw_ref[g]) * psc_ref[:, gs]
        yc_ref[:, gs] = y.astype(yc_ref.dtype)


def _mix_prompt_call(ug, vn, glu, p, ws, bs_full, cw, cb, lg, lb, pw, psc, layer):
    tm = MIX_TM
    blk = (tm, D_BR)
    last = T_P // tm - 1
    row_map = lambda i: (i, 0)
    in_map = lambda i: (jnp.minimum(i, last), 0)
    in_specs = [
        pl.BlockSpec(blk, in_map),
        pl.BlockSpec(blk, in_map),
        pl.BlockSpec(blk, in_map),
        pl.BlockSpec((CONV_HALO, D_BR), lambda i: (
            jnp.maximum(jnp.minimum(i, last) * (tm // CONV_HALO) - 1, 0), 0)),
        pl.BlockSpec(blk, in_map),
        pl.BlockSpec((POOL_HALO, D_BR), lambda i: (
            jnp.maximum(jnp.minimum(i, last) * (tm // POOL_HALO) - 1, 0), 0)),
        _layer_spec((H_A, CHUNK, CHUNK), layer),
        _layer_spec((CHUNK, D_A), layer),
        _layer_spec((CONV_K, D_CONV), layer),
        _layer_spec((1, D_CONV), layer),
        _layer_spec((1, D_CONV), layer),
        _layer_spec((1, D_CONV), layer),
        _layer_spec((N_POOL, G_POOL, G_POOL), layer),
        _layer_spec((1, D_POOL), layer),
    ]
    out = jax.ShapeDtypeStruct((T, D_BR), BF16)
    return pl.pallas_call(
        _mix_prompt_kernel,
        out_shape=(out, out, out),
        grid=(T // tm,),
        in_specs=in_specs,
        out_specs=(pl.BlockSpec(blk, row_map),) * 3,
        scratch_shapes=[pltpu.VMEM((CONV_HALO + tm, D_CONV), F32),
                        pltpu.VMEM((SUBLANES - 1, CONV_HALO + tm, D_CONV), F32),
                        pltpu.VMEM((POOL_HALO + tm, D_POOL), F32),
                        pltpu.VMEM((CONV_K, MIX_RB, D_CONV), F32),
                        pltpu.VMEM((POOL_HALO + tm, D_POOL), F32)],
        compiler_params=_params(
            ("parallel",),
            [(blk, F32)] * 4 + [(blk, BF16)] * 3 + [((CONV_HALO + POOL_HALO, D_BR), F32),
                                                    ((CHUNK + CONV_K + 8, D_BR), F32)],
            resident=_nbytes((SUBLANES * (CONV_HALO + tm) + 2 * (POOL_HALO + tm)
                              + CONV_K * MIX_RB, D_BR), F32),
            temps=20 * _nbytes(blk, F32)),
        name="mix_prompt",
    )(ug, vn, glu, glu, p, p, ws, bs_full, cw, cb, lg, lb, pw, psc)


def _mix_sample_kernel(ug_ref, vn_ref, glu_ref, p_ref, cpast_ref, ppast_ref,
                       wsm_ref, bsm_ref, cw_ref, cb_ref, lg_ref, lb_ref, pw_ref, psc_ref,
                       ya_in, yb_in, yc_in, ya_ref, yb_ref, yc_ref, mix_scr):
    del ya_in, yb_in, yc_in
    cb, lg, lb = cb_ref[...], lg_ref[...], lb_ref[...]

    def rows(t):
        return slice(t * SB, (t + 1) * SB)

    def conv_src(j):
        return cpast_ref[j] if j < CONV_K - 1 else glu_ref[rows(j - (CONV_K - 1)), :]

    def pool_src(j, gs):
        return ppast_ref[j, :, gs] if j < POOL_PAST else p_ref[rows(j - POOL_PAST), gs]

    for t in range(DEC_SEQ):
        m = bsm_ref[t:t + 1, :] + wsm_ref[t, 0:1, :] * vn_ref[rows(0), :]
        for s in range(1, t + 1):
            m = m + wsm_ref[t, s:s + 1, :] * vn_ref[rows(s), :]
        ya_ref[rows(t), :] = (ug_ref[rows(t), :] * m).astype(ya_ref.dtype)
        y = _conv_ln_silu(lambda k: conv_src(t + k), cw_ref, cb, lg, lb)
        yb_ref[rows(t), :] = y.astype(yb_ref.dtype)
        for g, wlen in enumerate(POOL_WINDOWS):
            gs = slice(g * G_POOL, (g + 1) * G_POOL)
            s = pool_src(POOL_PAST + t, gs)
            for d in range(1, wlen):
                s = s + pool_src(POOL_PAST + t - d, gs)
            cnt = float(min(wlen, PAST_LEN + t + 1))
            mix_scr[rows(t), gs] = s / cnt - p_ref[rows(t), gs]
    for g in range(N_POOL):
        gs = slice(g * G_POOL, (g + 1) * G_POOL)
        y = _dot(mix_scr[:, gs].astype(BF16), pw_ref[g]) * psc_ref[:, gs]
        yc_ref[:, gs] = y.astype(yc_ref.dtype)


def _mix_sample_call(ug, vn, glu, p, cpast, ppast, wsm, bsm, cw, cb, lg, lb, pw, psc, layer,
                     ya, yb, yc):
    blk = (SBLK, D_BR)
    blk0 = T_P // SBLK
    row_map = lambda j: (blk0 + j, 0)
    any_spec = pl.BlockSpec(memory_space=pl.ANY)
    in_specs = [
        pl.BlockSpec(blk, row_map), pl.BlockSpec(blk, row_map),
        pl.BlockSpec(blk, row_map), pl.BlockSpec(blk, row_map),
        pl.BlockSpec((None, CONV_K - 1, SB, D_CONV), lambda j: (layer, 0, j, 0)),
        pl.BlockSpec((None, POOL_PAST, SB, D_POOL), lambda j: (layer, 0, j, 0)),
        _layer_spec((DEC_SEQ, DEC_SEQ, D_A), layer),
        _layer_spec((DEC_SEQ, D_A), layer),
        _layer_spec((CONV_K, D_CONV), layer),
        _layer_spec((1, D_CONV), layer),
        _layer_spec((1, D_CONV), layer),
        _layer_spec((1, D_CONV), layer),
        _layer_spec((N_POOL, G_POOL, G_POOL), layer),
        _layer_spec((1, D_POOL), layer),
        any_spec, any_spec, any_spec,
    ]
    out = jax.ShapeDtypeStruct((T, D_BR), BF16)
    return pl.pallas_call(
        _mix_sample_kernel,
        out_shape=(out, out, out),
        grid=(N_SBLK,),
        in_specs=in_specs,
        out_specs=(pl.BlockSpec(blk, row_map),) * 3,
        scratch_shapes=[pltpu.VMEM(blk, F32)],
        input_output_aliases={14: 0, 15: 1, 16: 2},
        compiler_params=_params(
            ("parallel",),
            [(blk, F32)] * 4 + [(blk, BF16)] * 3
            + [((CONV_K - 1 + POOL_PAST, SB, D_BR), F32), ((CONV_K + 16, D_BR), F32)],
            resident=_nbytes(blk, F32), temps=32 * _nbytes(blk, F32)),
        name="mix_sample",
    )(ug, vn, glu, p, cpast, ppast, wsm, bsm, cw, cb, lg, lb, pw, psc, ya, yb, yc)


def _merge_kernel(n_prompt, ya_ref, yb_ref, yc_ref, g0_ref, g1_ref, g2_ref, *refs):
    if n_prompt is None:
        x_ref, wb_ref, wo_ref, gn_ref, xo_ref, ho_ref = refs
        x = x_ref[...]
    else:
        xp_ref, xs_ref, wb_ref, wo_ref, gn_ref, xo_ref, ho_ref = refs
        x = jnp.where(pl.program_id(0) < n_prompt, xp_ref[...], xs_ref[...])
    merged = g0_ref[...] * _dot(ya_ref[...], wb_ref[0])
    merged = merged + g1_ref[...] * _dot(yb_ref[...], wb_ref[1])
    merged = merged + g2_ref[...] * _dot(yc_ref[...], wb_ref[2])
    xn = x + _dot(merged.astype(BF16), wo_ref[...])
    xo_ref[...] = xn
    ho_ref[...] = _rms(xn, gn_ref[...]).astype(ho_ref.dtype)


def _merge_call(ya, yb, yc, gates, x, wb, wo, layer, gn, tm=256):
    yblk, xblk = (tm, D_BR), (tm, D_MODEL)
    row_map = lambda i: (i, 0)
    if isinstance(x, tuple):
        n_prompt = T_P // tm
        x_specs = [pl.BlockSpec(xblk, lambda i: (jnp.minimum(i, n_prompt - 1), 0)),
                   pl.BlockSpec(xblk, lambda i: (jnp.maximum(i - n_prompt, 0), 0))]
    else:
        n_prompt, x, x_specs = None, (x,), [pl.BlockSpec(xblk, row_map)]
    in_specs = [
        pl.BlockSpec(yblk, row_map), pl.BlockSpec(yblk, row_map), pl.BlockSpec(yblk, row_map),
        pl.BlockSpec(xblk, lambda i: (i, 0)),
        pl.BlockSpec(xblk, lambda i: (i, 1)),
        pl.BlockSpec(xblk, lambda i: (i, 2)),
        *x_specs,
        _resident((None, N_BRANCH, D_BR, D_MODEL), lambda i: (layer, 0, 0, 0)),
        _resident((None, D_MODEL, D_MODEL), lambda i: (layer, 0, 0)),
        _layer_spec((1, D_MODEL), layer),
    ]
    return pl.pallas_call(
        functools.partial(_merge_kernel, n_prompt),
        out_shape=(jax.ShapeDtypeStruct((T, D_MODEL), F32),
                   jax.ShapeDtypeStruct((T, D_MODEL), BF16)),
        grid=(T // tm,),
        in_specs=in_specs,
        out_specs=(pl.BlockSpec(xblk, row_map), pl.BlockSpec(xblk, row_map)),
        compiler_params=_params(
            ("parallel",),
            [(yblk, BF16)] * 3 + [(xblk, gates.dtype)] * 3 + [(xblk, F32)] * (1 + len(x))
            + [(xblk, BF16)],
            resident=_nbytes((N_BRANCH, D_BR, D_MODEL), BF16) + _nbytes((D_MODEL, D_MODEL), BF16),
            temps=3 * _nbytes(xblk, F32)),
        name="merge_out",
    )(ya, yb, yc, gates, gates, gates, *x, wb, wo, gn)


def _memkv_kernel(m_ref, g_ref, wk_ref, wv_ref, k32_ref, v32_ref, k16_ref, v16_ref):
    m = _rms(m_ref[...], g_ref[...]).astype(BF16)
    stride = HEAD_LANE_TILES * X_HEADS
    for w_ref, o32_ref, o16_ref in ((wk_ref, k32_ref, k16_ref), (wv_ref, v32_ref, v16_ref)):
        z = _dot(m, w_ref[...])
        o16_ref[...] = z.astype(BF16)
        for h in range(X_HEADS):
            for t in range(HEAD_LANE_TILES):
                c0 = h * X_HEAD_DIM + t * LANES
                o32_ref[pl.ds(t * X_HEADS + h, MEM_LEN, stride=stride), :] = z[:, c0:c0 + LANES]


def _memkv_call(mem, g, wk, wv):
    mblk, wblk = (MEM_LEN, D_MODEL), (None, D_MODEL, D_MODEL)
    o32 = jax.ShapeDtypeStruct((DEPTH, BATCH, KV_ROWS, LANES), F32)
    o16 = jax.ShapeDtypeStruct((DEPTH, BATCH * MEM_LEN, D_MODEL), BF16)
    o32_spec = pl.BlockSpec((None, None, KV_ROWS, LANES), lambda l, b: (l, b, 0, 0))
    o16_spec = pl.BlockSpec((None,) + mblk, lambda l, b: (l, b, 0))
    return pl.pallas_call(
        _memkv_kernel,
        out_shape=(o32, o32, o16, o16),
        grid=(DEPTH, BATCH),
        in_specs=[pl.BlockSpec(mblk, lambda l, b: (b, 0)),
                  pl.BlockSpec((None, 1, D_MODEL), lambda l, b: (l, 0, 0)),
                  pl.BlockSpec(wblk, lambda l, b: (l, 0, 0)),
                  pl.BlockSpec(wblk, lambda l, b: (l, 0, 0))],
        out_specs=(o32_spec, o32_spec, o16_spec, o16_spec),
        compiler_params=_params(
            ("parallel", "parallel"),
            [((D_MODEL, D_MODEL), BF16)] * 2 + [(mblk, F32)] * 3 + [(mblk, BF16)] * 2,
            temps=4 * _nbytes(mblk, F32)),
        name="mem_kv",
    )(mem, g, wk, wv)


def _rows_to_cache(c):
    d, b = c.shape[:2]
    c = c.reshape(d, b, MEM_LEN, HEAD_LANE_TILES, X_HEADS, LANES).transpose(0, 1, 2, 4, 3, 5)
    return c.reshape(d, b, MEM_LEN, X_HEADS, X_HEAD_DIM)


_NT = (((1,), (1,)), ((), ()))


def _softmax_rows(s):
    e = jnp.exp(s - jnp.max(s, axis=-1, keepdims=True))
    return e * (1.0 / jnp.sum(e, axis=-1, keepdims=True))


def _xattn_prompt_kernel(n_prompt, q_ref, k_ref, v_ref, o_ref):
    scale = X_HEAD_DIM ** -0.5
    is_prompt = pl.program_id(0) < n_prompt

    @pl.when(is_prompt)
    def _():
        for h in range(X_HEADS):
            hs = slice(h * X_HEAD_DIM, (h + 1) * X_HEAD_DIM)
            s = lax.dot_general(q_ref[:, hs], k_ref[:, hs], _NT,
                                preferred_element_type=F32) * scale
            pr = _softmax_rows(s).astype(BF16)
            o_ref[:, hs] = _dot(pr, v_ref[:, hs]).astype(o_ref.dtype)

    @pl.when(jnp.logical_not(is_prompt))
    def _():
        o_ref[...] = jnp.zeros_like(o_ref)


def _xattn_prompt_call(q, k16, v16, layer, tq=512):
    qblk, kblk = (tq, D_MODEL), (None, MEM_LEN, D_MODEL)
    per_seq = SEQ // tq
    n_prompt = T_P // tq
    batch_map = lambda i: (layer, jnp.minimum(i // per_seq, BATCH - 1), 0)
    return pl.pallas_call(
        functools.partial(_xattn_prompt_kernel, n_prompt),
        out_shape=jax.ShapeDtypeStruct((T, D_MODEL), BF16),
        grid=(T // tq,),
        in_specs=[pl.BlockSpec(qblk, lambda i: (i, 0)),
                  pl.BlockSpec(kblk, batch_map),
                  pl.BlockSpec(kblk, batch_map)],
        out_specs=pl.BlockSpec(qblk, lambda i: (i, 0)),
        compiler_params=_params(("parallel",),
                                [(qblk, BF16)] * 2 + [((MEM_LEN, D_MODEL), BF16)] * 2,
                                temps=4 * _nbytes((tq, MEM_LEN), F32) + _nbytes(qblk, F32)),
        name="xattn_prompt",
    )(q, k16, v16)


LANES = 128
HEAD_LANE_TILES = X_HEAD_DIM // LANES
KV_ROWS = MEM_LEN * HEAD_LANE_TILES * X_HEADS
KV_BB = 4


def _cache_rows(c):
    d, b = c.shape[:2]
    c = c.reshape(d, b, MEM_LEN, X_HEADS, HEAD_LANE_TILES, LANES).transpose(0, 1, 2, 4, 3, 5)
    return c.reshape(d, b, KV_ROWS, LANES)


def _head_rows(ref, i, h):
    stride = HEAD_LANE_TILES * X_HEADS
    tiles = [ref[i, pl.ds(t * X_HEADS + h, MEM_LEN, stride=stride), :]
             for t in range(HEAD_LANE_TILES)]
    return jnp.concatenate(tiles, axis=1)


def _xattn_sample_kernel(q_ref, k_ref, v_ref, o_in, o_ref, acc):
    del o_in
    c = pl.program_id(1)
    scale = X_HEAD_DIM ** -0.5

    @pl.when(c == 0)
    def _():
        acc[...] = jnp.zeros_like(acc)

    owner = lax.broadcasted_iota(jnp.int32, (SBLK, MEM_LEN), 0) % SB
    for h in range(X_HEADS):
        hs = slice(h * X_HEAD_DIM, (h + 1) * X_HEAD_DIM)
        keys = jnp.concatenate([_head_rows(k_ref, ii, h) for ii in range(KV_BB)], axis=0)
        s = lax.dot_general(q_ref[:, hs], keys.astype(BF16), _NT,
                            preferred_element_type=F32) * scale
        pr = [jnp.where(owner == c * KV_BB + ii,
                        _softmax_rows(s[:, ii * MEM_LEN:(ii + 1) * MEM_LEN]), 0.0)
              for ii in range(KV_BB)]
        vals = jnp.concatenate([_head_rows(v_ref, ii, h) for ii in range(KV_BB)], axis=0)
        acc[:, hs] += _dot(jnp.concatenate(pr, axis=1).astype(BF16), vals.astype(BF16))

    @pl.when(c == pl.num_programs(1) - 1)
    def _():
        o_ref[...] = acc[...].astype(o_ref.dtype)


def _xattn_sample_call(q, mk, mv, layer, o):
    qblk = (SBLK, D_MODEL)
    kblk = (None, KV_BB, KV_ROWS, LANES)
    blk0 = T_P // SBLK
    per_blk = SB // KV_BB
    return pl.pallas_call(
        _xattn_sample_kernel,
        out_shape=jax.ShapeDtypeStruct((T, D_MODEL), BF16),
        grid=(N_SBLK, per_blk),
        in_specs=[pl.BlockSpec(qblk, lambda j, c: (blk0 + j, 0)),
                  pl.BlockSpec(kblk, lambda j, c: (layer, j * per_blk + c, 0, 0)),
                  pl.BlockSpec(kblk, lambda j, c: (layer, j * per_blk + c, 0, 0)),
                  pl.BlockSpec(memory_space=pl.ANY)],
        out_specs=pl.BlockSpec(qblk, lambda j, c: (blk0 + j, 0)),
        scratch_shapes=[pltpu.VMEM(qblk, F32)],
        input_output_aliases={3: 0},
        compiler_params=_params(("parallel", "arbitrary"),
                                [(qblk, BF16)] * 2 + [((KV_BB, KV_ROWS, LANES), F32)] * 2,
                                resident=_nbytes(qblk, F32),
                                temps=4 * KV_BB * _nbytes((MEM_LEN, X_HEAD_DIM), F32)),
        name="xattn_sample",
    )(q, mk, mv, o)


def _resid_kernel(a_ref, w_ref, x_ref, gn_ref, xo_ref, ho_ref):
    xn = x_ref[...] + _dot(a_ref[...], w_ref[...])
    xo_ref[...] = xn
    ho_ref[...] = _rms(xn, gn_ref[...]).astype(ho_ref.dtype)


def _resid_call(a, w, layer, x, gn, tm=512):
    blk = (tm, D_MODEL)
    row_map = lambda i: (i, 0)
    return pl.pallas_call(
        _resid_kernel,
        out_shape=(jax.ShapeDtypeStruct((T, D_MODEL), F32),
                   jax.ShapeDtypeStruct((T, D_MODEL), BF16)),
        grid=(T // tm,),
        in_specs=[pl.BlockSpec(blk, row_map),
                  _resident((None, D_MODEL, D_MODEL), lambda i: (layer, 0, 0)),
                  pl.BlockSpec(blk, row_map),
                  _layer_spec((1, D_MODEL), layer)],
        out_specs=(pl.BlockSpec(blk, row_map), pl.BlockSpec(blk, row_map)),
        compiler_params=_params(("parallel",),
                                [(blk, BF16)] * 2 + [(blk, F32)] * 2,
                                resident=_nbytes((D_MODEL, D_MODEL), BF16),
                                temps=2 * _nbytes(blk, F32)),
        name="attn_out",
    )(a, w, x, gn)


def _mlp_kernel(n_prompt, h_ref, wu_ref, wd_ref, x_ref, gn_ref, o0_ref, o1_ref, acc):
    i, f = pl.program_id(0), pl.program_id(1)

    @pl.when(f == 0)
    def _():
        acc[...] = x_ref[...]

    hid = jnp.maximum(_dot(h_ref[...], wu_ref[...]), 0.0)
    acc[...] += _dot((hid * hid).astype(BF16), wd_ref[...])

    @pl.when(f == pl.num_programs(1) - 1)
    def _():
        xn = acc[...]
        if n_prompt is None:
            o0_ref[...] = xn
            o1_ref[...] = _rms(xn, gn_ref[...]).astype(o1_ref.dtype)
        else:
            y = _rms(xn, gn_ref[...])

            @pl.when(i < n_prompt)
            def _():
                o0_ref[...] = y

            @pl.when(i >= n_prompt)
            def _():
                o1_ref[...] = y


def _mlp_call(h, wu, wd, layer, x, gn, final, tm=512, tf=1024):
    blk = (tm, D_MODEL)
    row_map = lambda i, f: (i, 0)
    if final:
        n_prompt = T_P // tm
        out_shape = (jax.ShapeDtypeStruct((T_P, D_MODEL), F32),
                     jax.ShapeDtypeStruct((T_S, D_MODEL), F32))
        out_specs = (pl.BlockSpec(blk, lambda i, f: (jnp.minimum(i, n_prompt - 1), 0)),
                     pl.BlockSpec(blk, lambda i, f: (jnp.maximum(i - n_prompt, 0), 0)))
        out_dtype = F32
    else:
        n_prompt = None
        out_shape = (jax.ShapeDtypeStruct((T, D_MODEL), F32),
                     jax.ShapeDtypeStruct((T, D_MODEL), BF16))
        out_specs = (pl.BlockSpec(blk, row_map), pl.BlockSpec(blk, row_map))
        out_dtype = BF16
    return pl.pallas_call(
        functools.partial(_mlp_kernel, n_prompt),
        out_shape=out_shape,
        grid=(T // tm, D_FF // tf),
        in_specs=[pl.BlockSpec(blk, row_map),
                  pl.BlockSpec((None, D_MODEL, tf), lambda i, f: (layer, 0, f)),
                  pl.BlockSpec((None, tf, D_MODEL), lambda i, f: (layer, f, 0)),
                  pl.BlockSpec(blk, row_map),
                  _layer_spec((1, D_MODEL), layer)],
        out_specs=out_specs,
        scratch_shapes=[pltpu.VMEM(blk, F32)],
        compiler_params=_params(
            ("arbitrary", "arbitrary"),
            [(blk, BF16), ((D_MODEL, tf), BF16), ((tf, D_MODEL), BF16), (blk, F32), (blk, F32),
             (blk, out_dtype)],
            resident=_nbytes(blk, F32),
            temps=2 * _nbytes((tm, tf), F32) + _nbytes(blk, F32)),
        name="mlp_final" if final else "mlp",
    )(h, wu, wd, x, gn)


def _sample_to_rows(a):
    c = a.shape[-1]
    return a.reshape(N_SBLK, SB, DEC_SEQ, c).transpose(0, 2, 1, 3).reshape(T_S, c)


def _rows_to_sample(a):
    c = a.shape[-1]
    return a.reshape(N_SBLK, DEC_SEQ, SB, c).transpose(0, 2, 1, 3).reshape(DEC_BATCH, DEC_SEQ, c)


def kernel(x_prompt, x_sample, mem_prompt, cache_mem_k, cache_mem_v, state_conv, state_pool,
           g_mix, w_in, ln_v_g, ln_v_b, w_s, b_s, conv_w, conv_b, ln_c_g, ln_c_b,
           pool_w, pool_scale, w_branch, w_out, g_xattn, g_mem, w_xq, w_xk, w_xv, w_xo,
           g_mlp, w_up, w_down, g_final):
    assert T_S == 512 and T_P % 512 == 0
    rows = lambda a: a.reshape(DEPTH, 1, -1)
    mem = mem_prompt.reshape(BATCH * MEM_LEN, D_MODEL)
    cache_k, cache_v = _cache_rows(cache_mem_k), _cache_rows(cache_mem_v)
    wb16, wo16 = w_branch.astype(BF16), w_out.astype(BF16)
    wk16, wv16, wxo16 = w_xk.astype(BF16), w_xv.astype(BF16), w_xo.astype(BF16)
    wu16, wd16, pw = w_up.astype(BF16), w_down.astype(BF16), pool_w.astype(BF16)
    n1 = D_BR // 1024

    g_mix_r, g_xattn_r, g_mlp_r = rows(g_mix), rows(g_xattn), rows(g_mlp)
    g_next_r = rows(jnp.concatenate([g_mix[1:], g_final[None]], axis=0))
    lvg, lvb = rows(ln_v_g), rows(ln_v_b)
    cb, lg, lb, psc = rows(conv_b), rows(ln_c_g), rows(ln_c_b), rows(pool_scale)
    bs_full = jnp.repeat(b_s.transpose(0, 2, 1), DH_A, axis=2)
    wsm = jnp.repeat(w_s[:, :, :DEC_SEQ, :DEC_SEQ].transpose(0, 2, 3, 1), DH_A, axis=3)
    bsm = jnp.repeat(b_s[:, :, :DEC_SEQ].transpose(0, 2, 1), DH_A, axis=2)
    cpast = state_conv.transpose(0, 2, 1, 3)
    ppast = state_pool.transpose(0, 2, 1, 3)

    x = (x_prompt.reshape(T_P, D_MODEL), _sample_to_rows(x_sample))
    h = _norm_call(*x, g_mix_r)
    k32, v32, k16, v16 = _memkv_call(mem, rows(g_mem), wk16, wv16)
    conv_p, pool_p, glu_s, p_s, chunk_s = [], [], [], [], []
    y_prompt = y_sample = None
    for l in range(DEPTH):
        ug = _proj_call(h, w_in, l, col0=0, ncols=n1, mode="gelu", out_dtype=F32, name="proj_u")
        vn = _proj_call(h, w_in, l, col0=n1, ncols=n1, mode="gelu_ln", out_dtype=F32,
                        name="proj_v", extra=(lvg, lvb))
        glu = _proj_call(h, w_in, l, col0=2 * D_A // 512, ncols=D_CONV // 512, mode="glu",
                         out_dtype=F32, name="proj_glu", tn=512)
        p = _proj_call(h, w_in, l, col0=4 * n1, ncols=n1, mode="id", out_dtype=F32,
                       name="proj_p")
        gates = _proj_call(h, w_in, l, col0=5 * n1, ncols=N_BRANCH * D_MODEL // 1024,
                           mode="sigmoid", out_dtype=BF16, name="proj_gate")

        ya, yb, yc = _mix_prompt_call(ug, vn, glu, p, w_s, bs_full, conv_w, cb, lg, lb, pw, psc, l)
        ya, yb, yc = _mix_sample_call(ug, vn, glu, p, cpast, ppast, wsm, bsm, conv_w, cb, lg, lb,
                                      pw, psc, l, ya, yb, yc)

        x, hx = _merge_call(ya, yb, yc, gates, x, wb16, wo16, l, g_xattn_r)

        q = _proj_call(hx, w_xq, l, col0=0, ncols=D_MODEL // 1024, mode="id", out_dtype=BF16,
                       name="proj_q")
        o = _xattn_prompt_call(q, k16, v16, l)
        o = _xattn_sample_call(q, cache_k, cache_v, l, o)
        x, hm = _resid_call(o, wxo16, l, x, g_mlp_r)

        if l == DEPTH - 1:
            y_prompt, y_sample = _mlp_call(hm, wu16, wd16, l, x, g_next_r, True)
        else:
            x, h = _mlp_call(hm, wu16, wd16, l, x, g_next_r, False)

        conv_p.append(jnp.stack([glu[(b + 1) * SEQ - (CONV_K - 1):(b + 1) * SEQ]
                                 for b in range(BATCH)]))
        pool_p.append(jnp.stack([p[(b + 1) * SEQ - POOL_PAST:(b + 1) * SEQ]
                                 for b in range(BATCH)]))
        glu_s.append(_rows_to_sample(glu[T_P:]))
        p_s.append(_rows_to_sample(p[T_P:]))
        chunk_s.append(_rows_to_sample(vn[T_P:]))

    y_prompt = y_prompt.reshape(BATCH, SEQ, D_MODEL)
    y_sample = _rows_to_sample(y_sample)
    new_conv_s = jnp.concatenate([state_conv[:, :, DEC_SEQ:], jnp.stack(glu_s)], axis=2)
    new_pool_s = jnp.concatenate([state_pool[:, :, DEC_SEQ:], jnp.stack(p_s)], axis=2)
    return (y_prompt, y_sample, _rows_to_cache(k32), _rows_to_cache(v32), jnp.stack(conv_p),
            jnp.stack(pool_p), new_conv_s, new_pool_s, jnp.stack(chunk_s))
```

```python
import functools

import jax
import jax.numpy as jnp
from jax import lax
from jax.experimental import pallas as pl
from jax.experimental.pallas import tpu as pltpu

D_MODEL = 2048
BATCH = 4
SEQ = 2048
DEPTH = 2
DEC_BATCH = 128
DEC_SEQ = 4
PAST_LEN = 16384
D_BR = D_MODEL // 2
D_A = D_BR
H_A = 4
DH_A = D_A // H_A
CHUNK = 128
D_CONV = D_BR
CONV_K = 31
D_POOL = D_BR
POOL_WINDOWS = (2, 4, 8, 16)
N_POOL = len(POOL_WINDOWS)
G_POOL = D_POOL // N_POOL
POOL_PAST = max(POOL_WINDOWS) - 1
MEM_LEN = 256
X_HEADS = 4
X_HEAD_DIM = D_MODEL // X_HEADS
D_FF = 4 * D_MODEL
N_BRANCH = 3
D_IN = 2 * D_A + 2 * D_CONV + D_POOL + N_BRANCH * D_MODEL
RMS_EPS = 1e-6
LN_EPS = 1e-5

T_P = BATCH * SEQ
T_S = DEC_BATCH * DEC_SEQ
T = T_P + T_S
SB = 16
SBLK = DEC_SEQ * SB
N_SBLK = DEC_BATCH // SB

SUBLANES = 8
VMEM_PHYSICAL_BYTES = 64 * 1024 * 1024
VMEM_CEILING_BYTES = VMEM_PHYSICAL_BYTES - 6 * 1024 * 1024

F32 = jnp.float32
BF16 = jnp.bfloat16


def _nbytes(shape, dtype):
    n = 1
    for s in shape:
        n *= s
    return n * jnp.dtype(dtype).itemsize


def _params(semantics, pipelined, resident=0, temps=0):
    need = 2 * sum(_nbytes(s, d) for s, d in pipelined) + resident + temps
    need = need + need // 8 + (2 << 20)
    return pltpu.CompilerParams(
        dimension_semantics=semantics,
        vmem_limit_bytes=int(min(max(need, 16 << 20), VMEM_CEILING_BYTES)))


def _resident(shape, index_map):
    return pl.BlockSpec(shape, index_map, pipeline_mode=pl.Buffered(1))


def _layer_spec(shape, layer):
    zeros = (0,) * len(shape)
    return pl.BlockSpec((None,) + tuple(shape), lambda *_: (layer,) + zeros)


def _sigmoid(x):
    return 0.5 * jnp.tanh(0.5 * x) + 0.5


def _rms(x, g):
    return x * lax.rsqrt(jnp.mean(x * x, axis=-1, keepdims=True) + RMS_EPS) * g


def _ln(x, g, b):
    mu = jnp.mean(x, axis=-1, keepdims=True)
    xc = x - mu
    var = jnp.mean(xc * xc, axis=-1, keepdims=True)
    return xc * lax.rsqrt(var + LN_EPS) * g + b


def _dot(a, b):
    return jnp.dot(a, b, preferred_element_type=F32)


def _norm_kernel(n_prompt, xp_ref, xs_ref, g_ref, h_ref):
    def emit(src):
        h_ref[...] = _rms(src[...], g_ref[...]).astype(h_ref.dtype)

    @pl.when(pl.program_id(0) < n_prompt)
    def _():
        emit(xp_ref)

    @pl.when(pl.program_id(0) >= n_prompt)
    def _():
        emit(xs_ref)


def _norm_call(xp, xs, g, tm=512):
    n_prompt = T_P // tm
    blk = (tm, D_MODEL)
    return pl.pallas_call(
        functools.partial(_norm_kernel, n_prompt),
        out_shape=jax.ShapeDtypeStruct((T, D_MODEL), BF16),
        grid=(T // tm,),
        in_specs=[pl.BlockSpec(blk, lambda i: (jnp.minimum(i, n_prompt - 1), 0)),
                  pl.BlockSpec(blk, lambda i: (jnp.maximum(i - n_prompt, 0), 0)),
                  _layer_spec((1, D_MODEL), 0)],
        out_specs=pl.BlockSpec(blk, lambda i: (i, 0)),
        compiler_params=_params(("arbitrary",), [(blk, F32)] * 2 + [(blk, BF16)],
                                temps=2 * _nbytes(blk, F32)),
        name="rmsnorm_in",
    )(xp, xs, g)


def _proj_kernel(mode, *refs):
    n_w = 2 if mode == "glu" else 1
    n_extra = 2 if mode == "gelu_ln" else 0
    h_ref, w_refs = refs[0], refs[1:1 + n_w]
    extra = refs[1 + n_w:1 + n_w + n_extra]
    o_ref = refs[1 + n_w + n_extra]
    wbf = refs[2 + n_w + n_extra:]

    @pl.when(pl.program_id(1) == 0)
    def _():
        for src, dst in zip(w_refs, wbf):
            dst[...] = src[...].astype(BF16)

    h = h_ref[...]
    z = _dot(h, wbf[0][...])
    if mode == "glu":
        z = z * _sigmoid(_dot(h, wbf[1][...]))
    elif mode == "gelu":
        z = jax.nn.gelu(z)
    elif mode == "gelu_ln":
        z = _ln(jax.nn.gelu(z), extra[0][...], extra[1][...])
    elif mode == "sigmoid":
        z = _sigmoid(z)
    o_ref[...] = z.astype(o_ref.dtype)


def _proj_call(h, w, layer, *, col0, ncols, mode, out_dtype, name, extra=(), tm=1088, tn=1024):
    n, k = h.shape
    hblk, wblk, oblk = (tm, k), (k, tn), (tm, tn)
    n_w = 2 if mode == "glu" else 1
    in_specs = [pl.BlockSpec(hblk, lambda j, i: (i, 0)),
                pl.BlockSpec((None,) + wblk, lambda j, i: (layer, 0, col0 + j))]
    args = [h, w]
    if mode == "glu":
        in_specs.append(pl.BlockSpec((None,) + wblk, lambda j, i: (layer, 0, col0 + ncols + j)))
        args.append(w)
    for e in extra:
        in_specs.append(_layer_spec((1, tn), layer))
        args.append(e)
    return pl.pallas_call(
        functools.partial(_proj_kernel, mode),
        out_shape=jax.ShapeDtypeStruct((n, ncols * tn), out_dtype),
        grid=(ncols, n // tm),
        in_specs=in_specs,
        out_specs=pl.BlockSpec(oblk, lambda j, i: (i, j)),
        scratch_shapes=[pltpu.VMEM(wblk, BF16)] * n_w,
        compiler_params=_params(("arbitrary", "arbitrary"),
                                [(hblk, h.dtype), (oblk, out_dtype)] + [(wblk, F32)] * n_w,
                                resident=n_w * _nbytes(wblk, BF16),
                                temps=(2 + n_w) * _nbytes(oblk, F32)),
        name=name,
    )(*args)


MIX_TM = 512
MIX_RB = 16
CONV_HALO = 32
POOL_HALO = 16


def _conv_ln_silu(window, cw_ref, cb, lg, lb):
    acc = cb + cw_ref[0:1, :] * window(0)
    for k in range(1, CONV_K):
        acc = acc + cw_ref[k:k + 1, :] * window(k)
    y = _ln(acc, lg, lb)
    return y * _sigmoid(y)


def _mix_prompt_kernel(*refs):
    outs = refs[14:17]
    is_prompt = pl.program_id(0) < T_P // MIX_TM

    @pl.when(is_prompt)
    def _():
        _mix_prompt_tile(*refs)

    @pl.when(jnp.logical_not(is_prompt))
    def _():
        for o_ref in outs:
            o_ref[...] = jnp.zeros_like(o_ref)


def _mix_prompt_tile(ug_ref, vn_ref, glu_ref, gh_ref, p_ref, ph_ref,
                     ws_ref, bs_ref, cw_ref, cb_ref, lg_ref, lb_ref, pw_ref, psc_ref,
                     ya_ref, yb_ref, yc_ref, gwin, gshift, pwin, cwb, plev):
    tm = MIX_TM
    seq_tile = pl.program_id(0) % (SEQ // tm)
    first = seq_tile == 0

    row = lax.broadcasted_iota(jnp.int32, (CHUNK, CHUNK), 0)
    col = lax.broadcasted_iota(jnp.int32, (CHUNK, CHUNK), 1)
    for h in range(H_A):
        w = jnp.where(row >= col, ws_ref[h], 0.0).astype(BF16)
        hs = slice(h * DH_A, (h + 1) * DH_A)
        for c in range(tm // CHUNK):
            rs = slice(c * CHUNK, (c + 1) * CHUNK)
            mixed = _dot(w, vn_ref[rs, hs].astype(BF16)) + bs_ref[:, hs]
            ya_ref[rs, hs] = (ug_ref[rs, hs] * mixed).astype(ya_ref.dtype)

    gwin[0:CONV_HALO, :] = jnp.where(first, 0.0, gh_ref[...])
    gwin[CONV_HALO:, :] = glu_ref[...]
    n_shift = CONV_HALO + tm - SUBLANES
    for r in range(1, SUBLANES):
        gshift[r - 1, 0:n_shift, :] = gwin[r:r + n_shift, :]
    cb, lg, lb = cb_ref[...], lg_ref[...], lb_ref[...]
    off = CONV_HALO - (CONV_K - 1)

    def window(r0, k):
        a, r = divmod(off + k, SUBLANES)
        lo = r0 + SUBLANES * a
        if r == 0:
            return gwin[lo:lo + MIX_RB, :]
        return gshift[r - 1, lo:lo + MIX_RB, :]

    for k in range(CONV_K):
        cwb[k] = jnp.broadcast_to(cw_ref[k:k + 1, :], (MIX_RB, D_CONV))
    for rb in range(tm // MIX_RB):
        r0 = rb * MIX_RB
        acc = cb + cwb[0] * window(r0, 0)
        for k in range(1, CONV_K):
            acc = acc + cwb[k] * window(r0, k)
        y = _ln(acc, lg, lb)
        yb_ref[r0:r0 + MIX_RB, :] = (y * _sigmoid(y)).astype(yb_ref.dtype)

    pwin[0:POOL_HALO, :] = jnp.where(first, 0.0, ph_ref[...])
    pwin[POOL_HALO:, :] = p_ref[...]
    pos = seq_tile * tm + lax.broadcasted_iota(jnp.int32, (tm, 1), 0)
    for g, wlen in enumerate(POOL_WINDOWS):
        gs = slice(g * G_POOL, (g + 1) * G_POOL)
        src, span, n_rows = pwin, 1, POOL_HALO + tm
        while span < wlen:
            n_rows -= span
            plev[0:n_rows, gs] = src[0:n_rows, gs] + src[span:span + n_rows, gs]
            src, span = plev, 2 * span
        first_row = POOL_HALO + 1 - wlen
        s = src[first_row:first_row + tm, gs]
        inv = 1.0 / jnp.minimum(wlen, pos + 1).astype(F32)
        mixed = s * inv - p_ref[:, gs]
        y = _dot(mixed.astype(BF16), pw_ref[g]) * psc_ref[:, gs]
        yc_ref[:, gs] = y.astype(yc_ref.dtype)


def _mix_prompt_call(ug, vn, glu, p, ws, bs_full, cw, cb, lg, lb, pw, psc, layer):
    tm = MIX_TM
    blk = (tm, D_BR)
    last = T_P // tm - 1
    row_map = lambda i: (i, 0)
    in_map = lambda i: (jnp.minimum(i, last), 0)
    in_specs = [
        pl.BlockSpec(blk, in_map),
        pl.BlockSpec(blk, in_map),
        pl.BlockSpec(blk, in_map),
        pl.BlockSpec((CONV_HALO, D_BR), lambda i: (
            jnp.maximum(jnp.minimum(i, last) * (tm // CONV_HALO) - 1, 0), 0)),
        pl.BlockSpec(blk, in_map),
        pl.BlockSpec((POOL_HALO, D_BR), lambda i: (
            jnp.maximum(jnp.minimum(i, last) * (tm // POOL_HALO) - 1, 0), 0)),
        _layer_spec((H_A, CHUNK, CHUNK), layer),
        _layer_spec((CHUNK, D_A), layer),
        _layer_spec((CONV_K, D_CONV), layer),
        _layer_spec((1, D_CONV), layer),
        _layer_spec((1, D_CONV), layer),
        _layer_spec((1, D_CONV), layer),
        _layer_spec((N_POOL, G_POOL, G_POOL), layer),
        _layer_spec((1, D_POOL), layer),
    ]
    out = jax.ShapeDtypeStruct((T, D_BR), BF16)
    return pl.pallas_call(
        _mix_prompt_kernel,
        out_shape=(out, out, out),
        grid=(T // tm,),
        in_specs=in_specs,
        out_specs=(pl.BlockSpec(blk, row_map),) * 3,
        scratch_shapes=[pltpu.VMEM((CONV_HALO + tm, D_CONV), F32),
                        pltpu.VMEM((SUBLANES - 1, CONV_HALO + tm, D_CONV), F32),
                        pltpu.VMEM((POOL_HALO + tm, D_POOL), F32),
                        pltpu.VMEM((CONV_K, MIX_RB, D_CONV), F32),
                        pltpu.VMEM((POOL_HALO + tm, D_POOL), F32)],
        compiler_params=_params(
            ("parallel",),
            [(blk, F32)] * 4 + [(blk, BF16)] * 3 + [((CONV_HALO + POOL_HALO, D_BR), F32),
                                                    ((CHUNK + CONV_K + 8, D_BR), F32)],
            resident=_nbytes((SUBLANES * (CONV_HALO + tm) + 2 * (POOL_HALO + tm)
                              + CONV_K * MIX_RB, D_BR), F32),
            temps=20 * _nbytes(blk, F32)),
        name="mix_prompt",
    )(ug, vn, glu, glu, p, p, ws, bs_full, cw, cb, lg, lb, pw, psc)


def _mix_sample_kernel(ug_ref, vn_ref, glu_ref, p_ref, cpast_ref, ppast_ref,
                       wsm_ref, bsm_ref, cw_ref, cb_ref, lg_ref, lb_ref, pw_ref, psc_ref,
                       ya_in, yb_in, yc_in, ya_ref, yb_ref, yc_ref, mix_scr):
    del ya_in, yb_in, yc_in
    cb, lg, lb = cb_ref[...], lg_ref[...], lb_ref[...]

    def rows(t):
        return slice(t * SB, (t + 1) * SB)

    def conv_src(j):
        return cpast_ref[j] if j < CONV_K - 1 else glu_ref[rows(j - (CONV_K - 1)), :]

    def pool_src(j, gs):
        return ppast_ref[j, :, gs] if j < POOL_PAST else p_ref[rows(j - POOL_PAST), gs]

    for t in range(DEC_SEQ):
        m = bsm_ref[t:t + 1, :] + wsm_ref[t, 0:1, :] * vn_ref[rows(0), :]
        for s in range(1, t + 1):
            m = m + wsm_ref[t, s:s + 1, :] * vn_ref[rows(s), :]
        ya_ref[rows(t), :] = (ug_ref[rows(t), :] * m).astype(ya_ref.dtype)
        y = _conv_ln_silu(lambda k: conv_src(t + k), cw_ref, cb, lg, lb)
        yb_ref[rows(t), :] = y.astype(yb_ref.dtype)
        for g, wlen in enumerate(POOL_WINDOWS):
            gs = slice(g * G_POOL, (g + 1) * G_POOL)
            s = pool_src(POOL_PAST + t, gs)
            for d in range(1, wlen):
                s = s + pool_src(POOL_PAST + t - d, gs)
            cnt = float(min(wlen, PAST_LEN + t + 1))
            mix_scr[rows(t), gs] = s / cnt - p_ref[rows(t), gs]
    for g in range(N_POOL):
        gs = slice(g * G_POOL, (g + 1) * G_POOL)
        y = _dot(mix_scr[:, gs].astype(BF16), pw_ref[g]) * psc_ref[:, gs]
        yc_ref[:, gs] = y.astype(yc_ref.dtype)


def _mix_sample_call(ug, vn, glu, p, cpast, ppast, wsm, bsm, cw, cb, lg, lb, pw, psc, layer,
                     ya, yb, yc):
    blk = (SBLK, D_BR)
    blk0 = T_P // SBLK
    row_map = lambda j: (blk0 + j, 0)
    any_spec = pl.BlockSpec(memory_space=pl.ANY)
    in_specs = [
        pl.BlockSpec(blk, row_map), pl.BlockSpec(blk, row_map),
        pl.BlockSpec(blk, row_map), pl.BlockSpec(blk, row_map),
        pl.BlockSpec((None, CONV_K - 1, SB, D_CONV), lambda j: (layer, 0, j, 0)),
        pl.BlockSpec((None, POOL_PAST, SB, D_POOL), lambda j: (layer, 0, j, 0)),
        _layer_spec((DEC_SEQ, DEC_SEQ, D_A), layer),
        _layer_spec((DEC_SEQ, D_A), layer),
        _layer_spec((CONV_K, D_CONV), layer),
        _layer_spec((1, D_CONV), layer),
        _layer_spec((1, D_CONV), layer),
        _layer_spec((1, D_CONV), layer),
        _layer_spec((N_POOL, G_POOL, G_POOL), layer),
        _layer_spec((1, D_POOL), layer),
        any_spec, any_spec, any_spec,
    ]
    out = jax.ShapeDtypeStruct((T, D_BR), BF16)
    return pl.pallas_call(
        _mix_sample_kernel,
        out_shape=(out, out, out),
        grid=(N_SBLK,),
        in_specs=in_specs,
        out_specs=(pl.BlockSpec(blk, row_map),) * 3,
        scratch_shapes=[pltpu.VMEM(blk, F32)],
        input_output_aliases={14: 0, 15: 1, 16: 2},
        compiler_params=_params(
            ("parallel",),
            [(blk, F32)] * 4 + [(blk, BF16)] * 3
            + [((CONV_K - 1 + POOL_PAST, SB, D_BR), F32), ((CONV_K + 16, D_BR), F32)],
            resident=_nbytes(blk, F32), temps=32 * _nbytes(blk, F32)),
        name="mix_sample",
    )(ug, vn, glu, p, cpast, ppast, wsm, bsm, cw, cb, lg, lb, pw, psc, ya, yb, yc)


def _merge_kernel(n_prompt, ya_ref, yb_ref, yc_ref, g0_ref, g1_ref, g2_ref, *refs):
    if n_prompt is None:
        x_ref, wb_ref, wo_ref, gn_ref, xo_ref, ho_ref = refs
        x = x_ref[...]
    else:
        xp_ref, xs_ref, wb_ref, wo_ref, gn_ref, xo_ref, ho_ref = refs
        x = jnp.where(pl.program_id(0) < n_prompt, xp_ref[...], xs_ref[...])
    merged = g0_ref[...] * _dot(ya_ref[...], wb_ref[0])
    merged = merged + g1_ref[...] * _dot(yb_ref[...], wb_ref[1])
    merged = merged + g2_ref[...] * _dot(yc_ref[...], wb_ref[2])
    xn = x + _dot(merged.astype(BF16), wo_ref[...])
    xo_ref[...] = xn
    ho_ref[...] = _rms(xn, gn_ref[...]).astype(ho_ref.dtype)


def _merge_call(ya, yb, yc, gates, x, wb, wo, layer, gn, tm=256):
    yblk, xblk = (tm, D_BR), (tm, D_MODEL)
    row_map = lambda i: (i, 0)
    if isinstance(x, tuple):
        n_prompt = T_P // tm
        x_specs = [pl.BlockSpec(xblk, lambda i: (jnp.minimum(i, n_prompt - 1), 0)),
                   pl.BlockSpec(xblk, lambda i: (jnp.maximum(i - n_prompt, 0), 0))]
    else:
        n_prompt, x, x_specs = None, (x,), [pl.BlockSpec(xblk, row_map)]
    in_specs = [
        pl.BlockSpec(yblk, row_map), pl.BlockSpec(yblk, row_map), pl.BlockSpec(yblk, row_map),
        pl.BlockSpec(xblk, lambda i: (i, 0)),
        pl.BlockSpec(xblk, lambda i: (i, 1)),
        pl.BlockSpec(xblk, lambda i: (i, 2)),
        *x_specs,
        _resident((None, N_BRANCH, D_BR, D_MODEL), lambda i: (layer, 0, 0, 0)),
        _resident((None, D_MODEL, D_MODEL), lambda i: (layer, 0, 0)),
        _layer_spec((1, D_MODEL), layer),
    ]
    return pl.pallas_call(
        functools.partial(_merge_kernel, n_prompt),
        out_shape=(jax.ShapeDtypeStruct((T, D_MODEL), F32),
                   jax.ShapeDtypeStruct((T, D_MODEL), BF16)),
        grid=(T // tm,),
        in_specs=in_specs,
        out_specs=(pl.BlockSpec(xblk, row_map), pl.BlockSpec(xblk, row_map)),
        compiler_params=_params(
            ("parallel",),
            [(yblk, BF16)] * 3 + [(xblk, gates.dtype)] * 3 + [(xblk, F32)] * (1 + len(x))
            + [(xblk, BF16)],
            resident=_nbytes((N_BRANCH, D_BR, D_MODEL), BF16) + _nbytes((D_MODEL, D_MODEL), BF16),
            temps=3 * _nbytes(xblk, F32)),
        name="merge_out",
    )(ya, yb, yc, gates, gates, gates, *x, wb, wo, gn)


def _memkv_kernel(m_ref, g_ref, wk_ref, wv_ref, k32_ref, v32_ref, k16_ref, v16_ref):
    m = _rms(m_ref[...], g_ref[...]).astype(BF16)
    stride = HEAD_LANE_TILES * X_HEADS
    for w_ref, o32_ref, o16_ref in ((wk_ref, k32_ref, k16_ref), (wv_ref, v32_ref, v16_ref)):
        z = _dot(m, w_ref[...])
        o16_ref[...] = z.astype(BF16)
        for h in range(X_HEADS):
            for t in range(HEAD_LANE_TILES):
                c0 = h * X_HEAD_DIM + t * LANES
                o32_ref[pl.ds(t * X_HEADS + h, MEM_LEN, stride=stride), :] = z[:, c0:c0 + LANES]


def _memkv_call(mem, g, wk, wv):
    mblk, wblk = (MEM_LEN, D_MODEL), (None, D_MODEL, D_MODEL)
    o32 = jax.ShapeDtypeStruct((DEPTH, BATCH, KV_ROWS, LANES), F32)
    o16 = jax.ShapeDtypeStruct((DEPTH, BATCH * MEM_LEN, D_MODEL), BF16)
    o32_spec = pl.BlockSpec((None, None, KV_ROWS, LANES), lambda l, b: (l, b, 0, 0))
    o16_spec = pl.BlockSpec((None,) + mblk, lambda l, b: (l, b, 0))
    return pl.pallas_call(
        _memkv_kernel,
        out_shape=(o32, o32, o16, o16),
        grid=(DEPTH, BATCH),
        in_specs=[pl.BlockSpec(mblk, lambda l, b: (b, 0)),
                  pl.BlockSpec((None, 1, D_MODEL), lambda l, b: (l, 0, 0)),
                  pl.BlockSpec(wblk, lambda l, b: (l, 0, 0)),
                  pl.BlockSpec(wblk, lambda l, b: (l, 0, 0))],
        out_specs=(o32_spec, o32_spec, o16_spec, o16_spec),
        compiler_params=_params(
            ("parallel", "parallel"),
            [((D_MODEL, D_MODEL), BF16)] * 2 + [(mblk, F32)] * 3 + [(mblk, BF16)] * 2,
            temps=4 * _nbytes(mblk, F32)),
        name="mem_kv",
    )(mem, g, wk, wv)


def _rows_to_cache(c):
    d, b = c.shape[:2]
    c = c.reshape(d, b, MEM_LEN, HEAD_LANE_TILES, X_HEADS, LANES).transpose(0, 1, 2, 4, 3, 5)
    return c.reshape(d, b, MEM_LEN, X_HEADS, X_HEAD_DIM)


_NT = (((1,), (1,)), ((), ()))


def _softmax_rows(s):
    e = jnp.exp(s - jnp.max(s, axis=-1, keepdims=True))
    return e * (1.0 / jnp.sum(e, axis=-1, keepdims=True))


def _xattn_prompt_kernel(n_prompt, q_ref, k_ref, v_ref, o_ref):
    scale = X_HEAD_DIM ** -0.5
    is_prompt = pl.program_id(0) < n_prompt

    @pl.when(is_prompt)
    def _():
        for h in range(X_HEADS):
            hs = slice(h * X_HEAD_DIM, (h + 1) * X_HEAD_DIM)
            s = lax.dot_general(q_ref[:, hs], k_ref[:, hs], _NT,
                                preferred_element_type=F32) * scale
            pr = _softmax_rows(s).astype(BF16)
            o_ref[:, hs] = _dot(pr, v_ref[:, hs]).astype(o_ref.dtype)

    @pl.when(jnp.logical_not(is_prompt))
    def _():
        o_ref[...] = jnp.zeros_like(o_ref)


def _xattn_prompt_call(q, k16, v16, layer, tq=512):
    qblk, kblk = (tq, D_MODEL), (None, MEM_LEN, D_MODEL)
    per_seq = SEQ // tq
    n_prompt = T_P // tq
    batch_map = lambda i: (layer, jnp.minimum(i // per_seq, BATCH - 1), 0)
    return pl.pallas_call(
        functools.partial(_xattn_prompt_kernel, n_prompt),
        out_shape=jax.ShapeDtypeStruct((T, D_MODEL), BF16),
        grid=(T // tq,),
        in_specs=[pl.BlockSpec(qblk, lambda i: (i, 0)),
                  pl.BlockSpec(kblk, batch_map),
                  pl.BlockSpec(kblk, batch_map)],
        out_specs=pl.BlockSpec(qblk, lambda i: (i, 0)),
        compiler_params=_params(("parallel",),
                                [(qblk, BF16)] * 2 + [((MEM_LEN, D_MODEL), BF16)] * 2,
                                temps=4 * _nbytes((tq, MEM_LEN), F32) + _nbytes(qblk, F32)),
        name="xattn_prompt",
    )(q, k16, v16)


LANES = 128
HEAD_LANE_TILES = X_HEAD_DIM // LANES
KV_ROWS = MEM_LEN * HEAD_LANE_TILES * X_HEADS
KV_BB = 4


def _cache_rows(c):
    d, b = c.shape[:2]
    c = c.reshape(d, b, MEM_LEN, X_HEADS, HEAD_LANE_TILES, LANES).transpose(0, 1, 2, 4, 3, 5)
    return c.reshape(d, b, KV_ROWS, LANES)


def _head_rows(ref, i, h):
    stride = HEAD_LANE_TILES * X_HEADS
    tiles = [ref[i, pl.ds(t * X_HEADS + h, MEM_LEN, stride=stride), :]
             for t in range(HEAD_LANE_TILES)]
    return jnp.concatenate(tiles, axis=1)


def _xattn_sample_kernel(q_ref, k_ref, v_ref, o_in, o_ref, acc):
    del o_in
    c = pl.program_id(1)
    scale = X_HEAD_DIM ** -0.5

    @pl.when(c == 0)
    def _():
        acc[...] = jnp.zeros_like(acc)

    owner = lax.broadcasted_iota(jnp.int32, (SBLK, MEM_LEN), 0) % SB
    for h in range(X_HEADS):
        hs = slice(h * X_HEAD_DIM, (h + 1) * X_HEAD_DIM)
        keys = jnp.concatenate([_head_rows(k_ref, ii, h) for ii in range(KV_BB)], axis=0)
        s = lax.dot_general(q_ref[:, hs], keys.astype(BF16), _NT,
                            preferred_element_type=F32) * scale
        pr = [jnp.where(owner == c * KV_BB + ii,
                        _softmax_rows(s[:, ii * MEM_LEN:(ii + 1) * MEM_LEN]), 0.0)
              for ii in range(KV_BB)]
        vals = jnp.concatenate([_head_rows(v_ref, ii, h) for ii in range(KV_BB)], axis=0)
        acc[:, hs] += _dot(jnp.concatenate(pr, axis=1).astype(BF16), vals.astype(BF16))

    @pl.when(c == pl.num_programs(1) - 1)
    def _():
        o_ref[...] = acc[...].astype(o_ref.dtype)


def _xattn_sample_call(q, mk, mv, layer, o):
    qblk = (SBLK, D_MODEL)
    kblk = (None, KV_BB, KV_ROWS, LANES)
    blk0 = T_P // SBLK
    per_blk = SB // KV_BB
    return pl.pallas_call(
        _xattn_sample_kernel,
        out_shape=jax.ShapeDtypeStruct((T, D_MODEL), BF16),
        grid=(N_SBLK, per_blk),
        in_specs=[pl.BlockSpec(qblk, lambda j, c: (blk0 + j, 0)),
                  pl.BlockSpec(kblk, lambda j, c: (layer, j * per_blk + c, 0, 0)),
                  pl.BlockSpec(kblk, lambda j, c: (layer, j * per_blk + c, 0, 0)),
                  pl.BlockSpec(memory_space=pl.ANY)],
        out_specs=pl.BlockSpec(qblk, lambda j, c: (blk0 + j, 0)),
        scratch_shapes=[pltpu.VMEM(qblk, F32)],
        input_output_aliases={3: 0},
        compiler_params=_params(("parallel", "arbitrary"),
                                [(qblk, BF16)] * 2 + [((KV_BB, KV_ROWS, LANES), F32)] * 2,
                                resident=_nbytes(qblk, F32),
                                temps=4 * KV_BB * _nbytes((MEM_LEN, X_HEAD_DIM), F32)),
        name="xattn_sample",
    )(q, mk, mv, o)


def _resid_kernel(a_ref, w_ref, x_ref, gn_ref, xo_ref, ho_ref):
    xn = x_ref[...] + _dot(a_ref[...], w_ref[...])
    xo_ref[...] = xn
    ho_ref[...] = _rms(xn, gn_ref[...]).astype(ho_ref.dtype)


def _resid_call(a, w, layer, x, gn, tm=512):
    blk = (tm, D_MODEL)
    row_map = lambda i: (i, 0)
    return pl.pallas_call(
        _resid_kernel,
        out_shape=(jax.ShapeDtypeStruct((T, D_MODEL), F32),
                   jax.ShapeDtypeStruct((T, D_MODEL), BF16)),
        grid=(T // tm,),
        in_specs=[pl.BlockSpec(blk, row_map),
                  _resident((None, D_MODEL, D_MODEL), lambda i: (layer, 0, 0)),
                  pl.BlockSpec(blk, row_map),
                  _layer_spec((1, D_MODEL), layer)],
        out_specs=(pl.BlockSpec(blk, row_map), pl.BlockSpec(blk, row_map)),
        compiler_params=_params(("parallel",),
                                [(blk, BF16)] * 2 + [(blk, F32)] * 2,
                                resident=_nbytes((D_MODEL, D_MODEL), BF16),
                                temps=2 * _nbytes(blk, F32)),
        name="attn_out",
    )(a, w, x, gn)


def _mlp_kernel(n_prompt, h_ref, wu_ref, wd_ref, x_ref, gn_ref, o0_ref, o1_ref, acc):
    i, f = pl.program_id(0), pl.program_id(1)

    @pl.when(f == 0)
    def _():
        acc[...] = x_ref[...]

    hid = jnp.maximum(_dot(h_ref[...], wu_ref[...]), 0.0)
    acc[...] += _dot((hid * hid).astype(BF16), wd_ref[...])

    @pl.when(f == pl.num_programs(1) - 1)
    def _():
        xn = acc[...]
        if n_prompt is None:
            o0_ref[...] = xn
            o1_ref[...] = _rms(xn, gn_ref[...]).astype(o1_ref.dtype)
        else:
            y = _rms(xn, gn_ref[...])

            @pl.when(i < n_prompt)
            def _():
                o0_ref[...] = y

            @pl.when(i >= n_prompt)
            def _():
                o1_ref[...] = y


def _mlp_call(h, wu, wd, layer, x, gn, final, tm=512, tf=1024):
    blk = (tm, D_MODEL)
    row_map = lambda i, f: (i, 0)
    if final:
        n_prompt = T_P // tm
        out_shape = (jax.ShapeDtypeStruct((T_P, D_MODEL), F32),
                     jax.ShapeDtypeStruct((T_S, D_MODEL), F32))
        out_specs = (pl.BlockSpec(blk, lambda i, f: (jnp.minimum(i, n_prompt - 1), 0)),
                     pl.BlockSpec(blk, lambda i, f: (jnp.maximum(i - n_prompt, 0), 0)))
        out_dtype = F32
    else:
        n_prompt = None
        out_shape = (jax.ShapeDtypeStruct((T, D_MODEL), F32),
                     jax.ShapeDtypeStruct((T, D_MODEL), BF16))
        out_specs = (pl.BlockSpec(blk, row_map), pl.BlockSpec(blk, row_map))
        out_dtype = BF16
    return pl.pallas_call(
        functools.partial(_mlp_kernel, n_prompt),
        out_shape=out_shape,
        grid=(T // tm, D_FF // tf),
        in_specs=[pl.BlockSpec(blk, row_map),
                  pl.BlockSpec((None, D_MODEL, tf), lambda i, f: (layer, 0, f)),
                  pl.BlockSpec((None, tf, D_MODEL), lambda i, f: (layer, f, 0)),
                  pl.BlockSpec(blk, row_map),
                  _layer_spec((1, D_MODEL), layer)],
        out_specs=out_specs,
        scratch_shapes=[pltpu.VMEM(blk, F32)],
        compiler_params=_params(
            ("arbitrary", "arbitrary"),
            [(blk, BF16), ((D_MODEL, tf), BF16), ((tf, D_MODEL), BF16), (blk, F32), (blk, F32),
             (blk, out_dtype)],
            resident=_nbytes(blk, F32),
            temps=2 * _nbytes((tm, tf), F32) + _nbytes(blk, F32)),
        name="mlp_final" if final else "mlp",
    )(h, wu, wd, x, gn)


def _sample_to_rows(a):
    c = a.shape[-1]
    return a.reshape(N_SBLK, SB, DEC_SEQ, c).transpose(0, 2, 1, 3).reshape(T_S, c)


def _rows_to_sample(a):
    c = a.shape[-1]
    return a.reshape(N_SBLK, DEC_SEQ, SB, c).transpose(0, 2, 1, 3).reshape(DEC_BATCH, DEC_SEQ, c)


def kernel(x_prompt, x_sample, mem_prompt, cache_mem_k, cache_mem_v, state_conv, state_pool,
           g_mix, w_in, ln_v_g, ln_v_b, w_s, b_s, conv_w, conv_b, ln_c_g, ln_c_b,
           pool_w, pool_scale, w_branch, w_out, g_xattn, g_mem, w_xq, w_xk, w_xv, w_xo,
           g_mlp, w_up, w_down, g_final):
    assert T_S == 512 and T_P % 512 == 0
    rows = lambda a: a.reshape(DEPTH, 1, -1)
    mem = mem_prompt.reshape(BATCH * MEM_LEN, D_MODEL)
    cache_k, cache_v = _cache_rows(cache_mem_k), _cache_rows(cache_mem_v)
    wb16, wo16 = w_branch.astype(BF16), w_out.astype(BF16)
    wk16, wv16, wxo16 = w_xk.astype(BF16), w_xv.astype(BF16), w_xo.astype(BF16)
    wu16, wd16, pw = w_up.astype(BF16), w_down.astype(BF16), pool_w.astype(BF16)
    n1 = D_BR // 1024

    g_mix_r, g_xattn_r, g_mlp_r = rows(g_mix), rows(g_xattn), rows(g_mlp)
    g_next_r = rows(jnp.concatenate([g_mix[1:], g_final[None]], axis=0))
    lvg, lvb = rows(ln_v_g), rows(ln_v_b)
    cb, lg, lb, psc = rows(conv_b), rows(ln_c_g), rows(ln_c_b), rows(pool_scale)
    bs_full = jnp.repeat(b_s.transpose(0, 2, 1), DH_A, axis=2)
    wsm = jnp.repeat(w_s[:, :, :DEC_SEQ, :DEC_SEQ].transpose(0, 2, 3, 1), DH_A, axis=3)
    bsm = jnp.repeat(b_s[:, :, :DEC_SEQ].transpose(0, 2, 1), DH_A, axis=2)
    cpast = state_conv.transpose(0, 2, 1, 3)
    ppast = state_pool.transpose(0, 2, 1, 3)

    x = (x_prompt.reshape(T_P, D_MODEL), _sample_to_rows(x_sample))
    h = _norm_call(*x, g_mix_r)
    k32, v32, k16, v16 = _memkv_call(mem, rows(g_mem), wk16, wv16)
    conv_p, pool_p, glu_s, p_s, chunk_s = [], [], [], [], []
    y_prompt = y_sample = None
    for l in range(DEPTH):
        ug = _proj_call(h, w_in, l, col0=0, ncols=n1, mode="gelu", out_dtype=F32, name="proj_u")
        vn = _proj_call(h, w_in, l, col0=n1, ncols=n1, mode="gelu_ln", out_dtype=F32,
                        name="proj_v", extra=(lvg, lvb))
        glu = _proj_call(h, w_in, l, col0=2 * D_A // 512, ncols=D_CONV // 512, mode="glu",
                         out_dtype=F32, name="proj_glu", tn=512)
        p = _proj_call(h, w_in, l, col0=4 * n1, ncols=n1, mode="id", out_dtype=F32,
                       name="proj_p")
        gates = _proj_call(h, w_in, l, col0=5 * n1, ncols=N_BRANCH * D_MODEL // 1024,
                           mode="sigmoid", out_dtype=BF16, name="proj_gate")

        ya, yb, yc = _mix_prompt_call(ug, vn, glu, p, w_s, bs_full, conv_w, cb, lg, lb, pw, psc, l)
        ya, yb, yc = _mix_sample_call(ug, vn, glu, p, cpast, ppast, wsm, bsm, conv_w, cb, lg, lb,
                                      pw, psc, l, ya, yb, yc)

        x, hx = _merge_call(ya, yb, yc, gates, x, wb16, wo16, l, g_xattn_r)

        q = _proj_call(hx, w_xq, l, col0=0, ncols=D_MODEL // 1024, mode="id", out_dtype=BF16,
                       name="proj_q")
        o = _xattn_prompt_call(q, k16, v16, l)
        o = _xattn_sample_call(q, cache_k, cache_v, l, o)
        x, hm = _resid_call(o, wxo16, l, x, g_mlp_r)

        if l == DEPTH - 1:
            y_prompt, y_sample = _mlp_call(hm, wu16, wd16, l, x, g_next_r, True)
        else:
            x, h = _mlp_call(hm, wu16, wd16, l, x, g_next_r, False)

        conv_p.append(jnp.stack([glu[(b + 1) * SEQ - (CONV_K - 1):(b + 1) * SEQ]
                                 for b in range(BATCH)]))
        pool_p.append(jnp.stack([p[(b + 1) * SEQ - POOL_PAST:(b + 1) * SEQ]
                                 for b in range(BATCH)]))
        glu_s.append(_rows_to_sample(glu[T_P:]))
        p_s.append(_rows_to_sample(p[T_P:]))
        chunk_s.append(_rows_to_sample(vn[T_P:]))

    y_prompt = y_prompt.reshape(BATCH, SEQ, D_MODEL)
    y_sample = _rows_to_sample(y_sample)
    new_conv_s = jnp.concatenate([state_conv[:, :, DEC_SEQ:], jnp.stack(glu_s)], axis=2)
    new_pool_s = jnp.concatenate([state_pool[:, :, DEC_SEQ:], jnp.stack(p_s)], axis=2)
    return (y_prompt, y_sample, _rows_to_cache(k32), _rows_to_cache(v32), jnp.stack(conv_p),
            jnp.stack(pool_p), new_conv_s, new_pool_s, jnp.stack(chunk_s))
```

```python
import functools

import jax
import jax.numpy as jnp
from jax import lax
from jax.experimental import pallas as pl
from jax.experimental.pallas import tpu as pltpu

D_MODEL = 2048
BATCH = 4
SEQ = 2048
DEPTH = 2
DEC_BATCH = 128
DEC_SEQ = 4
PAST_LEN = 16384
D_BR = D_MODEL // 2
D_A = D_BR
H_A = 4
DH_A = D_A // H_A
CHUNK = 128
D_CONV = D_BR
CONV_K = 31
D_POOL = D_BR
POOL_WINDOWS = (2, 4, 8, 16)
N_POOL = len(POOL_WINDOWS)
G_POOL = D_POOL // N_POOL
POOL_PAST = max(POOL_WINDOWS) - 1
MEM_LEN = 256
X_HEADS = 4
X_HEAD_DIM = D_MODEL // X_HEADS
D_FF = 4 * D_MODEL
N_BRANCH = 3
D_IN = 2 * D_A + 2 * D_CONV + D_POOL + N_BRANCH * D_MODEL
RMS_EPS = 1e-6
LN_EPS = 1e-5

T_P = BATCH * SEQ
T_S = DEC_BATCH * DEC_SEQ
T = T_P + T_S
SB = 16
SBLK = DEC_SEQ * SB
N_SBLK = DEC_BATCH // SB

SUBLANES = 8
VMEM_PHYSICAL_BYTES = 64 * 1024 * 1024
VMEM_CEILING_BYTES = VMEM_PHYSICAL_BYTES - 3 * 1024 * 1024

F32 = jnp.float32
BF16 = jnp.bfloat16


def _nbytes(shape, dtype):
    n = 1
    for s in shape:
        n *= s
    return n * jnp.dtype(dtype).itemsize


def _params(semantics, pipelined, resident=0, temps=0):
    need = 2 * sum(_nbytes(s, d) for s, d in pipelined) + resident + temps
    need = need + need // 8 + (2 << 20)
    return pltpu.CompilerParams(
        dimension_semantics=semantics,
        vmem_limit_bytes=int(min(max(need, 16 << 20), VMEM_CEILING_BYTES)))


def _resident(shape, index_map):
    return pl.BlockSpec(shape, index_map, pipeline_mode=pl.Buffered(1))


def _layer_spec(shape, layer):
    zeros = (0,) * len(shape)
    return pl.BlockSpec((None,) + tuple(shape), lambda *_: (layer,) + zeros)


def _sigmoid(x):
    return 0.5 * jnp.tanh(0.5 * x) + 0.5


def _rms(x, g):
    return x * lax.rsqrt(jnp.mean(x * x, axis=-1, keepdims=True) + RMS_EPS) * g


def _ln(x, g, b):
    mu = jnp.mean(x, axis=-1, keepdims=True)
    xc = x - mu
    var = jnp.mean(xc * xc, axis=-1, keepdims=True)
    return xc * lax.rsqrt(var + LN_EPS) * g + b


def _dot(a, b):
    return jnp.dot(a, b, preferred_element_type=F32)


def _norm_kernel(n_prompt, xp_ref, xs_ref, g_ref, h_ref):
    def emit(src):
        h_ref[...] = _rms(src[...], g_ref[...]).astype(h_ref.dtype)

    @pl.when(pl.program_id(0) < n_prompt)
    def _():
        emit(xp_ref)

    @pl.when(pl.program_id(0) >= n_prompt)
    def _():
        emit(xs_ref)


def _norm_call(xp, xs, g, tm=512):
    n_prompt = T_P // tm
    blk = (tm, D_MODEL)
    return pl.pallas_call(
        functools.partial(_norm_kernel, n_prompt),
        out_shape=jax.ShapeDtypeStruct((T, D_MODEL), BF16),
        grid=(T // tm,),
        in_specs=[pl.BlockSpec(blk, lambda i: (jnp.minimum(i, n_prompt - 1), 0)),
                  pl.BlockSpec(blk, lambda i: (jnp.maximum(i - n_prompt, 0), 0)),
                  _layer_spec((1, D_MODEL), 0)],
        out_specs=pl.BlockSpec(blk, lambda i: (i, 0)),
        compiler_params=_params(("arbitrary",), [(blk, F32)] * 2 + [(blk, BF16)],
                                temps=2 * _nbytes(blk, F32)),
        name="rmsnorm_in",
    )(xp, xs, g)


def _proj_kernel(mode, *refs):
    n_w = 2 if mode == "glu" else 1
    n_extra = 2 if mode == "gelu_ln" else 0
    h_ref, w_refs = refs[0], refs[1:1 + n_w]
    extra = refs[1 + n_w:1 + n_w + n_extra]
    o_ref = refs[1 + n_w + n_extra]
    wbf = refs[2 + n_w + n_extra:]

    @pl.when(pl.program_id(1) == 0)
    def _():
        for src, dst in zip(w_refs, wbf):
            dst[...] = src[...].astype(BF16)

    h = h_ref[...]
    z = _dot(h, wbf[0][...])
    if mode == "glu":
        z = z * _sigmoid(_dot(h, wbf[1][...]))
    elif mode == "gelu":
        z = jax.nn.gelu(z)
    elif mode == "gelu_ln":
        z = _ln(jax.nn.gelu(z), extra[0][...], extra[1][...])
    elif mode == "sigmoid":
        z = _sigmoid(z)
    o_ref[...] = z.astype(o_ref.dtype)


def _proj_call(h, w, layer, *, col0, ncols, mode, out_dtype, name, extra=(), tm=1088, tn=1024):
    n, k = h.shape
    hblk, wblk, oblk = (tm, k), (k, tn), (tm, tn)
    n_w = 2 if mode == "glu" else 1
    in_specs = [pl.BlockSpec(hblk, lambda j, i: (i, 0)),
                pl.BlockSpec((None,) + wblk, lambda j, i: (layer, 0, col0 + j))]
    args = [h, w]
    if mode == "glu":
        in_specs.append(pl.BlockSpec((None,) + wblk, lambda j, i: (layer, 0, col0 + ncols + j)))
        args.append(w)
    for e in extra:
        in_specs.append(_layer_spec((1, tn), layer))
        args.append(e)
    return pl.pallas_call(
        functools.partial(_proj_kernel, mode),
        out_shape=jax.ShapeDtypeStruct((n, ncols * tn), out_dtype),
        grid=(ncols, n // tm),
        in_specs=in_specs,
        out_specs=pl.BlockSpec(oblk, lambda j, i: (i, j)),
        scratch_shapes=[pltpu.VMEM(wblk, BF16)] * n_w,
        compiler_params=_params(("arbitrary", "arbitrary"),
                                [(hblk, h.dtype), (oblk, out_dtype)] + [(wblk, F32)] * n_w,
                                resident=n_w * _nbytes(wblk, BF16),
                                temps=(2 + n_w) * _nbytes(oblk, F32)),
        name=name,
    )(*args)


MIX_TM = 512
MIX_RB = 16
CONV_HALO = 32
POOL_HALO = 16


def _conv_ln_silu(window, cw_ref, cb, lg, lb):
    acc = cb + cw_ref[0:1, :] * window(0)
    for k in range(1, CONV_K):
        acc = acc + cw_ref[k:k + 1, :] * window(k)
    y = _ln(acc, lg, lb)
    return y * _sigmoid(y)


def _mix_prompt_kernel(*refs):
    outs = refs[14:17]
    is_prompt = pl.program_id(0) < T_P // MIX_TM

    @pl.when(is_prompt)
    def _():
        _mix_prompt_tile(*refs)

    @pl.when(jnp.logical_not(is_prompt))
    def _():
        for o_ref in outs:
            o_ref[...] = jnp.zeros_like(o_ref)


def _mix_prompt_tile(ug_ref, vn_ref, glu_ref, gh_ref, p_ref, ph_ref,
                     ws_ref, bs_ref, cw_ref, cb_ref, lg_ref, lb_ref, pw_ref, psc_ref,
                     ya_ref, yb_ref, yc_ref, gwin, gshift, pwin, cwb, plev):
    tm = MIX_TM
    seq_tile = pl.program_id(0) % (SEQ // tm)
    first = seq_tile == 0

    row = lax.broadcasted_iota(jnp.int32, (CHUNK, CHUNK), 0)
    col = lax.broadcasted_iota(jnp.int32, (CHUNK, CHUNK), 1)
    for h in range(H_A):
        w = jnp.where(row >= col, ws_ref[h], 0.0).astype(BF16)
        hs = slice(h * DH_A, (h + 1) * DH_A)
        for c in range(tm // CHUNK):
            rs = slice(c * CHUNK, (c + 1) * CHUNK)
            mixed = _dot(w, vn_ref[rs, hs].astype(BF16)) + bs_ref[:, hs]
            ya_ref[rs, hs] = (ug_ref[rs, hs] * mixed).astype(ya_ref.dtype)

    gwin[0:CONV_HALO, :] = jnp.where(first, 0.0, gh_ref[...])
    gwin[CONV_HALO:, :] = glu_ref[...]
    n_shift = CONV_HALO + tm - SUBLANES
    for r in range(1, SUBLANES):
        gshift[r - 1, 0:n_shift, :] = gwin[r:r + n_shift, :]
    cb, lg, lb = cb_ref[...], lg_ref[...], lb_ref[...]
    off = CONV_HALO - (CONV_K - 1)

    def window(r0, k):
        a, r = divmod(off + k, SUBLANES)
        lo = r0 + SUBLANES * a
        if r == 0:
            return gwin[lo:lo + MIX_RB, :]
        return gshift[r - 1, lo:lo + MIX_RB, :]

    for k in range(CONV_K):
        cwb[k] = jnp.broadcast_to(cw_ref[k:k + 1, :], (MIX_RB, D_CONV))
    for rb in range(tm // MIX_RB):
        r0 = rb * MIX_RB
        acc = cb + cwb[0] * window(r0, 0)
        for k in range(1, CONV_K):
            acc = acc + cwb[k] * window(r0, k)
        y = _ln(acc, lg, lb)
        yb_ref[r0:r0 + MIX_RB, :] = (y * _sigmoid(y)).astype(yb_ref.dtype)

    pwin[0:POOL_HALO, :] = jnp.where(first, 0.0, ph_ref[...])
    pwin[POOL_HALO:, :] = p_ref[...]
    pos = seq_tile * tm + lax.broadcasted_iota(jnp.int32, (tm, 1), 0)
    for g, wlen in enumerate(POOL_WINDOWS):
        gs = slice(g * G_POOL, (g + 1) * G_POOL)
        src, span, n_rows = pwin, 1, POOL_HALO + tm
        while span < wlen:
            n_rows -= span
            plev[0:n_rows, gs] = src[0:n_rows, gs] + src[span:span + n_rows, gs]
            src, span = plev, 2 * span
        first_row = POOL_HALO + 1 - wlen
        s = src[first_row:first_row + tm, gs]
        inv = 1.0 / jnp.minimum(wlen, pos + 1).astype(F32)
        mixed = s * inv - p_ref[:, gs]
        y = _dot(mixed.astype(BF16), pw_ref[g]) * psc_ref[:, gs]
        yc_ref[:, gs] = y.astype(yc_ref.dtype)


def _mix_prompt_call(ug, vn, glu, p, ws, bs_full, cw, cb, lg, lb, pw, psc, layer):
    tm = MIX_TM
    blk = (tm, D_BR)
    last = T_P // tm - 1
    row_map = lambda i: (i, 0)
    in_map = lambda i: (jnp.minimum(i, last), 0)
    in_specs = [
        pl.BlockSpec(blk, in_map),
        pl.BlockSpec(blk, in_map),
        pl.BlockSpec(blk, in_map),
        pl.BlockSpec((CONV_HALO, D_BR), lambda i: (
            jnp.maximum(jnp.minimum(i, last) * (tm // CONV_HALO) - 1, 0), 0)),
        pl.BlockSpec(blk, in_map),
        pl.BlockSpec((POOL_HALO, D_BR), lambda i: (
            jnp.maximum(jnp.minimum(i, last) * (tm // POOL_HALO) - 1, 0), 0)),
        _layer_spec((H_A, CHUNK, CHUNK), layer),
        _layer_spec((CHUNK, D_A), layer),
        _layer_spec((CONV_K, D_CONV), layer),
        _layer_spec((1, D_CONV), layer),
        _layer_spec((1, D_CONV), layer),
        _layer_spec((1, D_CONV), layer),
        _layer_spec((N_POOL, G_POOL, G_POOL), layer),
        _layer_spec((1, D_POOL), layer),
    ]
    out = jax.ShapeDtypeStruct((T, D_BR), BF16)
    return pl.pallas_call(
        _mix_prompt_kernel,
        out_shape=(out, out, out),
        grid=(T // tm,),
        in_specs=in_specs,
        out_specs=(pl.BlockSpec(blk, row_map),) * 3,
        scratch_shapes=[pltpu.VMEM((CONV_HALO + tm, D_CONV), F32),
                        pltpu.VMEM((SUBLANES - 1, CONV_HALO + tm, D_CONV), F32),
                        pltpu.VMEM((POOL_HALO + tm, D_POOL), F32),
                        pltpu.VMEM((CONV_K, MIX_RB, D_CONV), F32),
                        pltpu.VMEM((POOL_HALO + tm, D_POOL), F32)],
        compiler_params=_params(
            ("parallel",),
            [(blk, F32)] * 4 + [(blk, BF16)] * 3 + [((CONV_HALO + POOL_HALO, D_BR), F32),
                                                    ((CHUNK + CONV_K + 8, D_BR), F32)],
            resident=_nbytes((SUBLANES * (CONV_HALO + tm) + 2 * (POOL_HALO + tm)
                              + CONV_K * MIX_RB, D_BR), F32),
            temps=20 * _nbytes(blk, F32)),
        name="mix_prompt",
    )(ug, vn, glu, glu, p, p, ws, bs_full, cw, cb, lg, lb, pw, psc)


def _mix_sample_kernel(ug_ref, vn_ref, glu_ref, p_ref, cpast_ref, ppast_ref,
                       wsm_ref, bsm_ref, cw_ref, cb_ref, lg_ref, lb_ref, pw_ref, psc_ref,
                       ya_in, yb_in, yc_in, ya_ref, yb_ref, yc_ref, mix_scr):
    del ya_in, yb_in, yc_in
    cb, lg, lb = cb_ref[...], lg_ref[...], lb_ref[...]

    def rows(t):
        return slice(t * SB, (t + 1) * SB)

    def conv_src(j):
        return cpast_ref[j] if j < CONV_K - 1 else glu_ref[rows(j - (CONV_K - 1)), :]

    def pool_src(j, gs):
        return ppast_ref[j, :, gs] if j < POOL_PAST else p_ref[rows(j - POOL_PAST), gs]

    for t in range(DEC_SEQ):
        m = bsm_ref[t:t + 1, :] + wsm_ref[t, 0:1, :] * vn_ref[rows(0), :]
        for s in range(1, t + 1):
            m = m + wsm_ref[t, s:s + 1, :] * vn_ref[rows(s), :]
        ya_ref[rows(t), :] = (ug_ref[rows(t), :] * m).astype(ya_ref.dtype)
        y = _conv_ln_silu(lambda k: conv_src(t + k), cw_ref, cb, lg, lb)
        yb_ref[rows(t), :] = y.astype(yb_ref.dtype)
        for g, wlen in enumerate(POOL_WINDOWS):
            gs = slice(g * G_POOL, (g + 1) * G_POOL)
            s = pool_src(POOL_PAST + t, gs)
            for d in range(1, wlen):
                s = s + pool_src(POOL_PAST + t - d, gs)
            cnt = float(min(wlen, PAST_LEN + t + 1))
            mix_scr[rows(t), gs] = s / cnt - p_ref[rows(t), gs]
    for g in range(N_POOL):
        gs = slice(g * G_POOL, (g + 1) * G_POOL)
        y = _dot(mix_scr[:, gs].astype(BF16), pw_ref[g]) * psc_ref[:, gs]
        yc_ref[:, gs] = y.astype(yc_ref.dtype)


def _mix_sample_call(ug, vn, glu, p, cpast, ppast, wsm, bsm, cw, cb, lg, lb, pw, psc, layer,
                     ya, yb, yc):
    blk = (SBLK, D_BR)
    blk0 = T_P // SBLK
    row_map = lambda j: (blk0 + j, 0)
    any_spec = pl.BlockSpec(memory_space=pl.ANY)
    in_specs = [
        pl.BlockSpec(blk, row_map), pl.BlockSpec(blk, row_map),
        pl.BlockSpec(blk, row_map), pl.BlockSpec(blk, row_map),
        pl.BlockSpec((None, CONV_K - 1, SB, D_CONV), lambda j: (layer, 0, j, 0)),
        pl.BlockSpec((None, POOL_PAST, SB, D_POOL), lambda j: (layer, 0, j, 0)),
        _layer_spec((DEC_SEQ, DEC_SEQ, D_A), layer),
        _layer_spec((DEC_SEQ, D_A), layer),
        _layer_spec((CONV_K, D_CONV), layer),
        _layer_spec((1, D_CONV), layer),
        _layer_spec((1, D_CONV), layer),
        _layer_spec((1, D_CONV), layer),
        _layer_spec((N_POOL, G_POOL, G_POOL), layer),
        _layer_spec((1, D_POOL), layer),
        any_spec, any_spec, any_spec,
    ]
    out = jax.ShapeDtypeStruct((T, D_BR), BF16)
    return pl.pallas_call(
        _mix_sample_kernel,
        out_shape=(out, out, out),
        grid=(N_SBLK,),
        in_specs=in_specs,
        out_specs=(pl.BlockSpec(blk, row_map),) * 3,
        scratch_shapes=[pltpu.VMEM(blk, F32)],
        input_output_aliases={14: 0, 15: 1, 16: 2},
        compiler_params=_params(
            ("parallel",),
            [(blk, F32)] * 4 + [(blk, BF16)] * 3
            + [((CONV_K - 1 + POOL_PAST, SB, D_BR), F32), ((CONV_K + 16, D_BR), F32)],
            resident=_nbytes(blk, F32), temps=32 * _nbytes(blk, F32)),
        name="mix_sample",
    )(ug, vn, glu, p, cpast, ppast, wsm, bsm, cw, cb, lg, lb, pw, psc, ya, yb, yc)


def _merge_kernel(n_prompt, ya_ref, yb_ref, yc_ref, g0_ref, g1_ref, g2_ref, *refs):
    if n_prompt is None:
        x_ref, wb_ref, wo_ref, gn_ref, xo_ref, ho_ref = refs
        x = x_ref[...]
    else:
        xp_ref, xs_ref, wb_ref, wo_ref, gn_ref, xo_ref, ho_ref = refs
        x = jnp.where(pl.program_id(0) < n_prompt, xp_ref[...], xs_ref[...])
    merged = g0_ref[...] * _dot(ya_ref[...], wb_ref[0])
    merged = merged + g1_ref[...] * _dot(yb_ref[...], wb_ref[1])
    merged = merged + g2_ref[...] * _dot(yc_ref[...], wb_ref[2])
    xn = x + _dot(merged.astype(BF16), wo_ref[...])
    xo_ref[...] = xn
    ho_ref[...] = _rms(xn, gn_ref[...]).astype(ho_ref.dtype)


def _merge_call(ya, yb, yc, gates, x, wb, wo, layer, gn, tm=256):
    yblk, xblk = (tm, D_BR), (tm, D_MODEL)
    row_map = lambda i: (i, 0)
    if isinstance(x, tuple):
        n_prompt = T_P // tm
        x_specs = [pl.BlockSpec(xblk, lambda i: (jnp.minimum(i, n_prompt - 1), 0)),
                   pl.BlockSpec(xblk, lambda i: (jnp.maximum(i - n_prompt, 0), 0))]
    else:
        n_prompt, x, x_specs = None, (x,), [pl.BlockSpec(xblk, row_map)]
    in_specs = [
        pl.BlockSpec(yblk, row_map), pl.BlockSpec(yblk, row_map), pl.BlockSpec(yblk, row_map),
        pl.BlockSpec(xblk, lambda i: (i, 0)),
        pl.BlockSpec(xblk, lambda i: (i, 1)),
        pl.BlockSpec(xblk, lambda i: (i, 2)),
        *x_specs,
        _resident((None, N_BRANCH, D_BR, D_MODEL), lambda i: (layer, 0, 0, 0)),
        _resident((None, D_MODEL, D_MODEL), lambda i: (layer, 0, 0)),
        _layer_spec((1, D_MODEL), layer),
    ]
    return pl.pallas_call(
        functools.partial(_merge_kernel, n_prompt),
        out_shape=(jax.ShapeDtypeStruct((T, D_MODEL), F32),
                   jax.ShapeDtypeStruct((T, D_MODEL), BF16)),
        grid=(T // tm,),
        in_specs=in_specs,
        out_specs=(pl.BlockSpec(xblk, row_map), pl.BlockSpec(xblk, row_map)),
        compiler_params=_params(
            ("parallel",),
            [(yblk, BF16)] * 3 + [(xblk, gates.dtype)] * 3 + [(xblk, F32)] * (1 + len(x))
            + [(xblk, BF16)],
            resident=_nbytes((N_BRANCH, D_BR, D_MODEL), BF16) + _nbytes((D_MODEL, D_MODEL), BF16),
            temps=3 * _nbytes(xblk, F32)),
        name="merge_out",
    )(ya, yb, yc, gates, gates, gates, *x, wb, wo, gn)


def _memkv_kernel(m_ref, g_ref, wk_ref, wv_ref, k32_ref, v32_ref, k16_ref, v16_ref):
    m = _rms(m_ref[...], g_ref[...]).astype(BF16)
    stride = HEAD_LANE_TILES * X_HEADS
    for w_ref, o32_ref, o16_ref in ((wk_ref, k32_ref, k16_ref), (wv_ref, v32_ref, v16_ref)):
        z = _dot(m, w_ref[...])
        o16_ref[...] = z.astype(BF16)
        for h in range(X_HEADS):
            for t in range(HEAD_LANE_TILES):
                c0 = h * X_HEAD_DIM + t * LANES
                o32_ref[pl.ds(t * X_HEADS + h, MEM_LEN, stride=stride), :] = z[:, c0:c0 + LANES]


def _memkv_call(mem, g, wk, wv):
    mblk, wblk = (MEM_LEN, D_MODEL), (None, D_MODEL, D_MODEL)
    o32 = jax.ShapeDtypeStruct((DEPTH, BATCH, KV_ROWS, LANES), F32)
    o16 = jax.ShapeDtypeStruct((DEPTH, BATCH * MEM_LEN, D_MODEL), BF16)
    o32_spec = pl.BlockSpec((None, None, KV_ROWS, LANES), lambda l, b: (l, b, 0, 0))
    o16_spec = pl.BlockSpec((None,) + mblk, lambda l, b: (l, b, 0))
    return pl.pallas_call(
        _memkv_kernel,
        out_shape=(o32, o32, o16, o16),
        grid=(DEPTH, BATCH),
        in_specs=[pl.BlockSpec(mblk, lambda l, b: (b, 0)),
                  pl.BlockSpec((None, 1, D_MODEL), lambda l, b: (l, 0, 0)),
                  pl.BlockSpec(wblk, lambda l, b: (l, 0, 0)),
                  pl.BlockSpec(wblk, lambda l, b: (l, 0, 0))],
        out_specs=(o32_spec, o32_spec, o16_spec, o16_spec),
        compiler_params=_params(
            ("parallel", "parallel"),
            [((D_MODEL, D_MODEL), BF16)] * 2 + [(mblk, F32)] * 3 + [(mblk, BF16)] * 2,
            temps=4 * _nbytes(mblk, F32)),
        name="mem_kv",
    )(mem, g, wk, wv)


def _rows_to_cache(c):
    d, b = c.shape[:2]
    c = c.reshape(d, b, MEM_LEN, HEAD_LANE_TILES, X_HEADS, LANES).transpose(0, 1, 2, 4, 3, 5)
    return c.reshape(d, b, MEM_LEN, X_HEADS, X_HEAD_DIM)


_NT = (((1,), (1,)), ((), ()))


def _softmax_rows(s):
    e = jnp.exp(s - jnp.max(s, axis=-1, keepdims=True))
    return e * (1.0 / jnp.sum(e, axis=-1, keepdims=True))


def _xattn_prompt_kernel(n_prompt, q_ref, k_ref, v_ref, o_ref):
    scale = X_HEAD_DIM ** -0.5
    is_prompt = pl.program_id(0) < n_prompt

    @pl.when(is_prompt)
    def _():
        for h in range(X_HEADS):
            hs = slice(h * X_HEAD_DIM, (h + 1) * X_HEAD_DIM)
            s = lax.dot_general(q_ref[:, hs], k_ref[:, hs], _NT,
                                preferred_element_type=F32) * scale
            pr = _softmax_rows(s).astype(BF16)
            o_ref[:, hs] = _dot(pr, v_ref[:, hs]).astype(o_ref.dtype)

    @pl.when(jnp.logical_not(is_prompt))
    def _():
        o_ref[...] = jnp.zeros_like(o_ref)


def _xattn_prompt_call(q, k16, v16, layer, tq=512):
    qblk, kblk = (tq, D_MODEL), (None, MEM_LEN, D_MODEL)
    per_seq = SEQ // tq
    n_prompt = T_P // tq
    batch_map = lambda i: (layer, jnp.minimum(i // per_seq, BATCH - 1), 0)
    return pl.pallas_call(
        functools.partial(_xattn_prompt_kernel, n_prompt),
        out_shape=jax.ShapeDtypeStruct((T, D_MODEL), BF16),
        grid=(T // tq,),
        in_specs=[pl.BlockSpec(qblk, lambda i: (i, 0)),
                  pl.BlockSpec(kblk, batch_map),
                  pl.BlockSpec(kblk, batch_map)],
        out_specs=pl.BlockSpec(qblk, lambda i: (i, 0)),
        compiler_params=_params(("parallel",),
                                [(qblk, BF16)] * 2 + [((MEM_LEN, D_MODEL), BF16)] * 2,
                                temps=4 * _nbytes((tq, MEM_LEN), F32) + _nbytes(qblk, F32)),
        name="xattn_prompt",
    )(q, k16, v16)


LANES = 128
HEAD_LANE_TILES = X_HEAD_DIM // LANES
KV_ROWS = MEM_LEN * HEAD_LANE_TILES * X_HEADS
KV_BB = 4


def _cache_rows(c):
    d, b = c.shape[:2]
    c = c.reshape(d, b, MEM_LEN, X_HEADS, HEAD_LANE_TILES, LANES).transpose(0, 1, 2, 4, 3, 5)
    return c.reshape(d, b, KV_ROWS, LANES)


def _head_rows(ref, i, h):
    stride = HEAD_LANE_TILES * X_HEADS
    tiles = [ref[i, pl.ds(t * X_HEADS + h, MEM_LEN, stride=stride), :]
             for t in range(HEAD_LANE_TILES)]
    return jnp.concatenate(tiles, axis=1)


def _xattn_sample_kernel(q_ref, k_ref, v_ref, o_in, o_ref, acc):
    del o_in
    c = pl.program_id(1)
    scale = X_HEAD_DIM ** -0.5

    @pl.when(c == 0)
    def _():
        acc[...] = jnp.zeros_like(acc)

    owner = lax.broadcasted_iota(jnp.int32, (SBLK, MEM_LEN), 0) % SB
    for h in range(X_HEADS):
        hs = slice(h * X_HEAD_DIM, (h + 1) * X_HEAD_DIM)
        keys = jnp.concatenate([_head_rows(k_ref, ii, h) for ii in range(KV_BB)], axis=0)
        s = lax.dot_general(q_ref[:, hs], keys.astype(BF16), _NT,
                            preferred_element_type=F32) * scale
        pr = [jnp.where(owner == c * KV_BB + ii,
                        _softmax_rows(s[:, ii * MEM_LEN:(ii + 1) * MEM_LEN]), 0.0)
              for ii in range(KV_BB)]
        vals = jnp.concatenate([_head_rows(v_ref, ii, h) for ii in range(KV_BB)], axis=0)
        acc[:, hs] += _dot(jnp.concatenate(pr, axis=1).astype(BF16), vals.astype(BF16))

    @pl.when(c == pl.num_programs(1) - 1)
    def _():
        o_ref[...] = acc[...].astype(o_ref.dtype)


def _xattn_sample_call(q, mk, mv, layer, o):
    qblk = (SBLK, D_MODEL)
    kblk = (None, KV_BB, KV_ROWS, LANES)
    blk0 = T_P // SBLK
    per_blk = SB // KV_BB
    return pl.pallas_call(
        _xattn_sample_kernel,
        out_shape=jax.ShapeDtypeStruct((T, D_MODEL), BF16),
        grid=(N_SBLK, per_blk),
        in_specs=[pl.BlockSpec(qblk, lambda j, c: (blk0 + j, 0)),
                  pl.BlockSpec(kblk, lambda j, c: (layer, j * per_blk + c, 0, 0)),
                  pl.BlockSpec(kblk, lambda j, c: (layer, j * per_blk + c, 0, 0)),
                  pl.BlockSpec(memory_space=pl.ANY)],
        out_specs=pl.BlockSpec(qblk, lambda j, c: (blk0 + j, 0)),
        scratch_shapes=[pltpu.VMEM(qblk, F32)],
        input_output_aliases={3: 0},
        compiler_params=_params(("parallel", "arbitrary"),
                                [(qblk, BF16)] * 2 + [((KV_BB, KV_ROWS, LANES), F32)] * 2,
                                resident=_nbytes(qblk, F32),
                                temps=4 * KV_BB * _nbytes((MEM_LEN, X_HEAD_DIM), F32)),
        name="xattn_sample",
    )(q, mk, mv, o)


def _resid_kernel(a_ref, w_ref, x_ref, gn_ref, xo_ref, ho_ref):
    xn = x_ref[...] + _dot(a_ref[...], w_ref[...])
    xo_ref[...] = xn
    ho_ref[...] = _rms(xn, gn_ref[...]).astype(ho_ref.dtype)


def _resid_call(a, w, layer, x, gn, tm=512):
    blk = (tm, D_MODEL)
    row_map = lambda i: (i, 0)
    return pl.pallas_call(
        _resid_kernel,
        out_shape=(jax.ShapeDtypeStruct((T, D_MODEL), F32),
                   jax.ShapeDtypeStruct((T, D_MODEL), BF16)),
        grid=(T // tm,),
        in_specs=[pl.BlockSpec(blk, row_map),
                  _resident((None, D_MODEL, D_MODEL), lambda i: (layer, 0, 0)),
                  pl.BlockSpec(blk, row_map),
                  _layer_spec((1, D_MODEL), layer)],
        out_specs=(pl.BlockSpec(blk, row_map), pl.BlockSpec(blk, row_map)),
        compiler_params=_params(("parallel",),
                                [(blk, BF16)] * 2 + [(blk, F32)] * 2,
                                resident=_nbytes((D_MODEL, D_MODEL), BF16),
                                temps=2 * _nbytes(blk, F32)),
        name="attn_out",
    )(a, w, x, gn)


MLP_SLOTS = 3


def _mlp_kernel(n_prompt, n_steps, n_f, layer, h_ref, wu_hbm, wd_hbm, x_ref, gn_ref,
                o0_ref, o1_ref, acc, wu_buf, wd_buf, sem):
    i, f = pl.program_id(0), pl.program_id(1)
    tf = wu_buf.shape[2]
    step = i * n_f + f

    def chunk_copies(chunk, slot):
        col = pl.multiple_of(chunk * tf, tf)
        return (pltpu.make_async_copy(wu_hbm.at[layer, :, pl.ds(col, tf)], wu_buf.at[slot],
                                      sem.at[0, slot]),
                pltpu.make_async_copy(wd_hbm.at[layer, pl.ds(col, tf), :], wd_buf.at[slot],
                                      sem.at[1, slot]))

    @pl.when(step == 0)
    def _():
        for s in range(MLP_SLOTS - 1):
            for c in chunk_copies(s, s):
                c.start()

    @pl.when(f == 0)
    def _():
        acc[...] = x_ref[...]

    for slot in range(MLP_SLOTS):
        @pl.when(step % MLP_SLOTS == slot)
        def _():
            ahead = MLP_SLOTS - 1

            @pl.when(step + ahead < n_steps)
            def _():
                for c in chunk_copies((f + ahead) % n_f, (slot + ahead) % MLP_SLOTS):
                    c.start()

            for c in chunk_copies(f, slot):
                c.wait()
            hid = jnp.maximum(_dot(h_ref[...], wu_buf[slot]), 0.0)
            acc[...] += _dot((hid * hid).astype(BF16), wd_buf[slot])

    @pl.when(f == n_f - 1)
    def _():
        xn = acc[...]
        if n_prompt is None:
            o0_ref[...] = xn
            o1_ref[...] = _rms(xn, gn_ref[...]).astype(o1_ref.dtype)
        else:
            y = _rms(xn, gn_ref[...])

            @pl.when(i < n_prompt)
            def _():
                o0_ref[...] = y

            @pl.when(i >= n_prompt)
            def _():
                o1_ref[...] = y


def _mlp_call(h, wu, wd, layer, x, gn, final, tm=512, tf=1024):
    blk = (tm, D_MODEL)
    row_map = lambda i, f: (i, 0)
    if final:
        n_prompt = T_P // tm
        out_shape = (jax.ShapeDtypeStruct((T_P, D_MODEL), F32),
                     jax.ShapeDtypeStruct((T_S, D_MODEL), F32))
        out_specs = (pl.BlockSpec(blk, lambda i, f: (jnp.minimum(i, n_prompt - 1), 0)),
                     pl.BlockSpec(blk, lambda i, f: (jnp.maximum(i - n_prompt, 0), 0)))
        out_dtype = F32
    else:
        n_prompt = None
        out_shape = (jax.ShapeDtypeStruct((T, D_MODEL), F32),
                     jax.ShapeDtypeStruct((T, D_MODEL), BF16))
        out_specs = (pl.BlockSpec(blk, row_map), pl.BlockSpec(blk, row_map))
        out_dtype = BF16
    n_tiles, n_f = T // tm, D_FF // tf
    assert n_f >= MLP_SLOTS - 1
    ring = _nbytes((MLP_SLOTS, D_MODEL, tf), BF16) + _nbytes((MLP_SLOTS, tf, D_MODEL), BF16)
    return pl.pallas_call(
        functools.partial(_mlp_kernel, n_prompt, n_tiles * n_f, n_f, layer),
        out_shape=out_shape,
        grid=(n_tiles, n_f),
        in_specs=[pl.BlockSpec(blk, row_map),
                  pl.BlockSpec(memory_space=pl.ANY),
                  pl.BlockSpec(memory_space=pl.ANY),
                  pl.BlockSpec(blk, row_map),
                  _layer_spec((1, D_MODEL), layer)],
        out_specs=out_specs,
        scratch_shapes=[pltpu.VMEM(blk, F32),
                        pltpu.VMEM((MLP_SLOTS, D_MODEL, tf), BF16),
                        pltpu.VMEM((MLP_SLOTS, tf, D_MODEL), BF16),
                        pltpu.SemaphoreType.DMA((2, MLP_SLOTS))],
        compiler_params=_params(
            ("arbitrary", "arbitrary"),
            [(blk, BF16), (blk, F32), (blk, F32), (blk, out_dtype)],
            resident=_nbytes(blk, F32) + ring,
            temps=2 * _nbytes((tm, tf), F32) + _nbytes(blk, F32)),
        name="mlp_final" if final else "mlp",
    )(h, wu, wd, x, gn)


def _sample_to_rows(a):
    c = a.shape[-1]
    return a.reshape(N_SBLK, SB, DEC_SEQ, c).transpose(0, 2, 1, 3).reshape(T_S, c)


def _rows_to_sample(a):
    c = a.shape[-1]
    return a.reshape(N_SBLK, DEC_SEQ, SB, c).transpose(0, 2, 1, 3).reshape(DEC_BATCH, DEC_SEQ, c)


def kernel(x_prompt, x_sample, mem_prompt, cache_mem_k, cache_mem_v, state_conv, state_pool,
           g_mix, w_in, ln_v_g, ln_v_b, w_s, b_s, conv_w, conv_b, ln_c_g, ln_c_b,
           pool_w, pool_scale, w_branch, w_out, g_xattn, g_mem, w_xq, w_xk, w_xv, w_xo,
           g_mlp, w_up, w_down, g_final):
    assert T_S == 512 and T_P % 512 == 0
    rows = lambda a: a.reshape(DEPTH, 1, -1)
    mem = mem_prompt.reshape(BATCH * MEM_LEN, D_MODEL)
    cache_k, cache_v = _cache_rows(cache_mem_k), _cache_rows(cache_mem_v)
    wb16, wo16 = w_branch.astype(BF16), w_out.astype(BF16)
    wk16, wv16, wxo16 = w_xk.astype(BF16), w_xv.astype(BF16), w_xo.astype(BF16)
    wu16, wd16, pw = w_up.astype(BF16), w_down.astype(BF16), pool_w.astype(BF16)
    n1 = D_BR // 1024

    g_mix_r, g_xattn_r, g_mlp_r = rows(g_mix), rows(g_xattn), rows(g_mlp)
    g_next_r = rows(jnp.concatenate([g_mix[1:], g_final[None]], axis=0))
    lvg, lvb = rows(ln_v_g), rows(ln_v_b)
    cb, lg, lb, psc = rows(conv_b), rows(ln_c_g), rows(ln_c_b), rows(pool_scale)
    bs_full = jnp.repeat(b_s.transpose(0, 2, 1), DH_A, axis=2)
    wsm = jnp.repeat(w_s[:, :, :DEC_SEQ, :DEC_SEQ].transpose(0, 2, 3, 1), DH_A, axis=3)
    bsm = jnp.repeat(b_s[:, :, :DEC_SEQ].transpose(0, 2, 1), DH_A, axis=2)
    cpast = state_conv.transpose(0, 2, 1, 3)
    ppast = state_pool.transpose(0, 2, 1, 3)

    x = (x_prompt.reshape(T_P, D_MODEL), _sample_to_rows(x_sample))
    h = _norm_call(*x, g_mix_r)
    k32, v32, k16, v16 = _memkv_call(mem, rows(g_mem), wk16, wv16)
    conv_p, pool_p, glu_s, p_s, chunk_s = [], [], [], [], []
    y_prompt = y_sample = None
    for l in range(DEPTH):
        ug = _proj_call(h, w_in, l, col0=0, ncols=n1, mode="gelu", out_dtype=F32, name="proj_u")
        vn = _proj_call(h, w_in, l, col0=n1, ncols=n1, mode="gelu_ln", out_dtype=F32,
                        name="proj_v", extra=(lvg, lvb))
        glu = _proj_call(h, w_in, l, col0=2 * D_A // 512, ncols=D_CONV // 512, mode="glu",
                         out_dtype=F32, name="proj_glu", tn=512)
        p = _proj_call(h, w_in, l, col0=4 * n1, ncols=n1, mode="id", out_dtype=F32,
                       name="proj_p")
        gates = _proj_call(h, w_in, l, col0=5 * n1, ncols=N_BRANCH * D_MODEL // 1024,
                           mode="sigmoid", out_dtype=BF16, name="proj_gate")

        ya, yb, yc = _mix_prompt_call(ug, vn, glu, p, w_s, bs_full, conv_w, cb, lg, lb, pw, psc, l)
        ya, yb, yc = _mix_sample_call(ug, vn, glu, p, cpast, ppast, wsm, bsm, conv_w, cb, lg, lb,
                                      pw, psc, l, ya, yb, yc)

        x, hx = _merge_call(ya, yb, yc, gates, x, wb16, wo16, l, g_xattn_r)

        q = _proj_call(hx, w_xq, l, col0=0, ncols=D_MODEL // 1024, mode="id", out_dtype=BF16,
                       name="proj_q")
        o = _xattn_prompt_call(q, k16, v16, l)
        o = _xattn_sample_call(q, cache_k, cache_v, l, o)
        x, hm = _resid_call(o, wxo16, l, x, g_mlp_r)

        if l == DEPTH - 1:
            y_prompt, y_sample = _mlp_call(hm, wu16, wd16, l, x, g_next_r, True)
        else:
            x, h = _mlp_call(hm, wu16, wd16, l, x, g_next_r, False)

        conv_p.append(jnp.stack([glu[(b + 1) * SEQ - (CONV_K - 1):(b + 1) * SEQ]
                                 for b in range(BATCH)]))
        pool_p.append(jnp.stack([p[(b + 1) * SEQ - POOL_PAST:(b + 1) * SEQ]
                                 for b in range(BATCH)]))
        glu_s.append(_rows_to_sample(glu[T_P:]))
        p_s.append(_rows_to_sample(p[T_P:]))
        chunk_s.append(_rows_to_sample(vn[T_P:]))

    y_prompt = y_prompt.reshape(BATCH, SEQ, D_MODEL)
    y_sample = _rows_to_sample(y_sample)
    new_conv_s = jnp.concatenate([state_conv[:, :, DEC_SEQ:], jnp.stack(glu_s)], axis=2)
    new_pool_s = jnp.concatenate([state_pool[:, :, DEC_SEQ:], jnp.stack(p_s)], axis=2)
    return (y_prompt, y_sample, _rows_to_cache(k32), _rows_to_cache(v32), jnp.stack(conv_p),
            jnp.stack(pool_p), new_conv_s, new_pool_s, jnp.stack(chunk_s))
```
